```python
import math
import jax, jax.numpy as jnp
from jax import lax
import numpy as np

D_MODEL = 1024
BATCH = 4
SEQ = 4096
DEPTH = 4

CHUNK = 128
CONV_K = 4
NORM_EPS = 1e-6

ML_HEADS = 6
ML_HEAD_DIM = 128
ML_WIDTH = ML_HEADS * ML_HEAD_DIM
SSD_HEADS = 12
SSD_HEAD_DIM = 64
SSD_WIDTH = SSD_HEADS * SSD_HEAD_DIM
SSD_GROUPS = 2
SSD_STATE = 128
SSD_CONV_DIM = SSD_WIDTH + 2 * SSD_GROUPS * SSD_STATE
DT_MIN = 1e-3
DT_MAX = 1e-1
RET_HEADS = 4
RET_HEAD_DIM = 128
RET_WIDTH = RET_HEADS * RET_HEAD_DIM
ROPE_BASE = 10000.0
MAX_POS_OFFSET = 1024

MIX_WIDTH = ML_WIDTH + SSD_WIDTH + RET_WIDTH
ML_IN = 4 * ML_WIDTH + 2 * ML_HEADS
SSD_IN = SSD_WIDTH + SSD_CONV_DIM + SSD_HEADS
RET_IN = 4 * RET_WIDTH
IN_WIDTH = ML_IN + SSD_IN + RET_IN
FFN_HIDDEN = -(-8 * D_MODEL // (3 * 256)) * 256

kernel_name = "hymba_mlstm_ssd_retention_trunk"


def rmsnorm(x, g):
    xf = x.astype(jnp.float32)
    y = xf * lax.rsqrt(jnp.mean(xf * xf, axis=-1, keepdims=True) + NORM_EPS)
    return (y * g.astype(jnp.float32)).astype(x.dtype)


def group_norm(x, g, n_groups, center=False):
    shp = x.shape
    xf = x.astype(jnp.float32).reshape(*shp[:-1], n_groups, shp[-1] // n_groups)
    if center:
        xf = xf - jnp.mean(xf, axis=-1, keepdims=True)
    xf = xf * lax.rsqrt(jnp.mean(xf * xf, axis=-1, keepdims=True) + NORM_EPS)
    return (xf.reshape(shp) * g.astype(jnp.float32)).astype(x.dtype)


def causal_dwconv(x, w, b):
    y = lax.conv_general_dilated(
        x, w[:, None, :].astype(x.dtype), window_strides=(1,), padding=[(CONV_K - 1, 0)],
        dimension_numbers=('NWC', 'WIO', 'NWC'), feature_group_count=x.shape[-1])
    return y + b


def chunk_heads(t):
    t = t.reshape(t.shape[0], t.shape[1] // CHUNK, CHUNK, *t.shape[2:])
    return jnp.swapaxes(t, 2, 3)


def unchunk_heads(t):
    t = jnp.swapaxes(t, 2, 3)
    return t.reshape(t.shape[0], t.shape[1] * t.shape[2], *t.shape[3:])


def causal_mask():
    return jnp.tril(jnp.ones((CHUNK, CHUNK), dtype=bool))


def mlstm_chunkwise(q, k, v, i_pre, f_pre):
    f32 = jnp.float32
    d = q.shape[-1]
    qc = chunk_heads(q.astype(f32))
    kc = chunk_heads(k.astype(f32)) * (d ** -0.5)
    vc = chunk_heads(v.astype(f32))
    ig = chunk_heads(i_pre.astype(f32))
    lf = jax.nn.log_sigmoid(chunk_heads(f_pre.astype(f32)))
    b = jnp.cumsum(lf, axis=-1)
    b_last = b[..., -1]
    mask = causal_mask()
    log_d = jnp.where(mask, b[..., :, None] - b[..., None, :] + ig[..., None, :], -jnp.inf)
    w_state = b_last[..., None] - b + ig
    m_loc = jnp.max(w_state, axis=-1)
    e_state = jnp.exp(w_state - m_loc[..., None])
    c_loc = jnp.einsum('bchs,bchsk,bchsv->bchkv', e_state, kc, vc)
    n_loc = jnp.einsum('bchs,bchsk->bchk', e_state, kc)

    def step(carry, inp):
        c_st, n_st, m_st = carry
        c_l, n_l, m_l, bl = inp
        m_new = jnp.maximum(bl + m_st, m_l)
        a_old = jnp.exp(bl + m_st - m_new)
        a_loc = jnp.exp(m_l - m_new)
        c_new = a_old[..., None, None] * c_st + a_loc[..., None, None] * c_l
        n_new = a_old[..., None] * n_st + a_loc[..., None] * n_l
        return (c_new, n_new, m_new), (c_st, n_st, m_st)

    bsz, _, h = b_last.shape[0], b_last.shape[1], b_last.shape[2]
    init = (jnp.zeros((bsz, h, d, d), f32), jnp.zeros((bsz, h, d), f32), jnp.zeros((bsz, h), f32))
    xs = tuple(jnp.moveaxis(t, 1, 0) for t in (c_loc, n_loc, m_loc, b_last))
    _, (c_prev, n_prev, m_prev) = lax.scan(step, init, xs)
    c_prev = jnp.moveaxis(c_prev, 0, 1)
    n_prev = jnp.moveaxis(n_prev, 0, 1)
    m_prev = jnp.moveaxis(m_prev, 0, 1)

    a = b + m_prev[..., None]
    m_t = jnp.maximum(a, jnp.max(log_d, axis=-1))
    decay = jnp.exp(log_d - m_t[..., None])
    inter = jnp.exp(a - m_t)
    scores = jnp.einsum('bchtk,bchsk->bchts', qc, kc) * decay
    num = (jnp.einsum('bchts,bchsv->bchtv', scores, vc)
           + inter[..., None] * jnp.einsum('bchtk,bchkv->bchtv', qc, c_prev))
    den = jnp.sum(scores, axis=-1) + inter * jnp.einsum('bchtk,bchk->bcht', qc, n_prev)
    out = num / jnp.maximum(jnp.abs(den), jnp.exp(-m_t))[..., None]
    return unchunk_heads(out).astype(q.dtype)


def ssd_chunkwise(x, dt, a_neg, b_in, c_in):
    f32 = jnp.float32
    bsz, s, h, p = x.shape
    g = b_in.shape[2]
    hg = h // g
    nc = s // CHUNK
    xc = (x.astype(f32) * dt.astype(f32)[..., None]).reshape(bsz, nc, CHUNK, g, hg, p)
    a_c = (dt.astype(f32) * a_neg.astype(f32)).reshape(bsz, nc, CHUNK, g, hg)
    a_cum = jnp.cumsum(jnp.transpose(a_c, (0, 1, 3, 4, 2)), axis=-1)
    bc = b_in.astype(f32).reshape(bsz, nc, CHUNK, g, -1)
    cc = c_in.astype(f32).reshape(bsz, nc, CHUNK, g, -1)
    mask = causal_mask()
    l_dec = jnp.exp(jnp.where(mask, a_cum[..., :, None] - a_cum[..., None, :], -jnp.inf))
    cb = jnp.einsum('bclgn,bcsgn->bcgls', cc, bc)
    y_diag = jnp.einsum('bcgls,bcghls,bcsghp->bclghp', cb, l_dec, xc)
    decay_states = jnp.exp(a_cum[..., -1:] - a_cum)
    states = jnp.einsum('bcsgn,bcghs,bcsghp->bcghpn', bc, decay_states, xc)
    chunk_decay = jnp.exp(a_cum[..., -1])

    def step(carry, inp):
        st, dec = inp
        return dec[..., None, None] * carry + st, carry

    init = jnp.zeros((bsz, g, hg, p, bc.shape[-1]), f32)
    _, prev = lax.scan(step, init, (jnp.moveaxis(states, 1, 0), jnp.moveaxis(chunk_decay, 1, 0)))
    prev = jnp.moveaxis(prev, 0, 1)
    y_off = jnp.einsum('bclgn,bcghpn,bcghl->bclghp', cc, prev, jnp.exp(a_cum))
    return (y_diag + y_off).reshape(bsz, s, h, p).astype(x.dtype)


def retention_chunkwise(q, k, v, log_gamma):
    f32 = jnp.float32
    qc, kc, vc = (chunk_heads(t.astype(f32)) for t in (q, k, v))
    idx = jnp.arange(CHUNK, dtype=f32)
    mask = causal_mask()
    lg = log_gamma[:, None, None]
    d_mat = jnp.exp(jnp.where(mask, lg * (idx[:, None] - idx[None, :]), -jnp.inf))
    scores = jnp.einsum('bchtk,bchsk->bchts', qc, kc) * d_mat
    y_intra = jnp.einsum('bchts,bchsv->bchtv', scores, vc)
    zeta = jnp.exp(log_gamma[:, None] * (CHUNK - 1 - idx))
    xi = jnp.exp(log_gamma[:, None] * (idx + 1))
    states = jnp.einsum('bchsk,hs,bchsv->bchkv', kc, zeta, vc)
    chunk_decay = jnp.exp(log_gamma * CHUNK)[:, None, None]

    def step(carry, st):
        return chunk_decay * carry + st, carry

    init = jnp.zeros(states.shape[:1] + states.shape[2:], f32)
    _, r_prev = lax.scan(step, init, jnp.moveaxis(states, 1, 0))
    r_prev = jnp.moveaxis(r_prev, 0, 1)
    y_cross = jnp.einsum('bchtk,bchkv,ht->bchtv', qc, r_prev, xi)
    return unchunk_heads(y_intra + y_cross).astype(q.dtype)


def rope_tables(positions):
    inv_freq = ROPE_BASE ** (-jnp.arange(0, RET_HEAD_DIM, 2, dtype=jnp.float32) / RET_HEAD_DIM)
    ang = positions.astype(jnp.float32)[..., None] * inv_freq
    return jnp.cos(ang)[:, :, None, :], jnp.sin(ang)[:, :, None, :]


def apply_rope(t, cos, sin):
    tf = t.astype(jnp.float32)
    t1, t2 = jnp.split(tf, 2, axis=-1)
    return jnp.concatenate([t1 * cos - t2 * sin, t1 * sin + t2 * cos], axis=-1).astype(t.dtype)


def hybrid_layer(x, cos, sin, log_gamma, norm_mix, w_in, ml_conv_w, ml_conv_b, ml_gate_bias,
                 ml_norm, ssd_conv_w, ssd_conv_b, ssd_dt_bias, ssd_a_log, ssd_d, ssd_norm,
                 ret_norm, w_out, norm_ffn, w_gate_up, w_down):
    bsz, s, _ = x.shape
    h = rmsnorm(x, norm_mix)
    proj = h @ w_in
    ml_p, ssd_p, ret_p = jnp.split(proj, [ML_IN, ML_IN + SSD_IN], axis=-1)

    ml_qk, ml_v, ml_o, ml_if = jnp.split(ml_p, [2 * ML_WIDTH, 3 * ML_WIDTH, 4 * ML_WIDTH], axis=-1)
    ml_qk = jax.nn.silu(causal_dwconv(ml_qk, ml_conv_w, ml_conv_b))
    ml_q, ml_k = jnp.split(ml_qk, 2, axis=-1)
    i_pre, f_pre = jnp.split(ml_if + ml_gate_bias, 2, axis=-1)
    hml = mlstm_chunkwise(ml_q.reshape(bsz, s, ML_HEADS, ML_HEAD_DIM),
                          ml_k.reshape(bsz, s, ML_HEADS, ML_HEAD_DIM),
                          ml_v.reshape(bsz, s, ML_HEADS, ML_HEAD_DIM), i_pre, f_pre)
    y_ml = group_norm(hml.reshape(bsz, s, ML_WIDTH), ml_norm, ML_HEADS) * jax.nn.sigmoid(ml_o)

    z, xbc, dt_pre = jnp.split(ssd_p, [SSD_WIDTH, SSD_WIDTH + SSD_CONV_DIM], axis=-1)
    xbc = jax.nn.silu(causal_dwconv(xbc, ssd_conv_w, ssd_conv_b))
    xs, b_in, c_in = jnp.split(xbc, [SSD_WIDTH, SSD_WIDTH + SSD_GROUPS * SSD_STATE], axis=-1)
    dt = jax.nn.softplus(dt_pre + ssd_dt_bias)
    a_neg = -jnp.exp(ssd_a_log)
    xs_h = xs.reshape(bsz, s, SSD_HEADS, SSD_HEAD_DIM)
    y = ssd_chunkwise(xs_h, dt, a_neg, b_in.reshape(bsz, s, SSD_GROUPS, SSD_STATE),
                      c_in.reshape(bsz, s, SSD_GROUPS, SSD_STATE))
    y = y + ssd_d[:, None] * xs_h
    y_ssd = group_norm(y.reshape(bsz, s, SSD_WIDTH) * jax.nn.silu(z), ssd_norm, SSD_GROUPS)

    rq, rk, rv, rg = jnp.split(ret_p, 4, axis=-1)
    rq = apply_rope(rq.reshape(bsz, s, RET_HEADS, RET_HEAD_DIM), cos, sin)
    rk = apply_rope(rk.reshape(bsz, s, RET_HEADS, RET_HEAD_DIM), cos, sin) * (RET_HEAD_DIM ** -0.5)
    yr = retention_chunkwise(rq, rk, rv.reshape(bsz, s, RET_HEADS, RET_HEAD_DIM), log_gamma)
    y_ret = group_norm(yr.reshape(bsz, s, RET_WIDTH), ret_norm, RET_HEADS, center=True) * jax.nn.silu(rg)

    mix = jnp.concatenate([y_ml, y_ssd, y_ret], axis=-1)
    x = x + mix @ w_out

    h = rmsnorm(x, norm_ffn)
    gate, up = jnp.split(h @ w_gate_up, 2, axis=-1)
    return x + (jax.nn.silu(gate) * up) @ w_down


def setup_inputs(seed: int = 0) -> dict:
    key = jax.random.key(seed)
    ks = jax.random.split(key, 21)
    f32 = jnp.float32
    resid_scale = (2 * DEPTH) ** -0.5

    def nrm(k, shape, scale):
        return jax.random.normal(k, shape, f32) * scale

    x = nrm(ks[0], (BATCH, SEQ, D_MODEL), 1.0)
    offsets = jax.random.randint(ks[1], (BATCH, 1), 0, MAX_POS_OFFSET, dtype=jnp.int32)
    positions = offsets + jnp.arange(SEQ, dtype=jnp.int32)[None, :]
    norm_mix = 1.0 + nrm(ks[2], (DEPTH, D_MODEL), 0.02)
    w_in = nrm(ks[3], (DEPTH, D_MODEL, IN_WIDTH), D_MODEL ** -0.5)
    ml_conv_w = nrm(ks[4], (DEPTH, CONV_K, 2 * ML_WIDTH), CONV_K ** -0.5)
    ml_conv_b = nrm(ks[5], (DEPTH, 2 * ML_WIDTH), 0.02)
    i_bias = nrm(ks[6], (DEPTH, ML_HEADS), 0.5)
    f_bias = 3.0 + 3.0 * jax.random.uniform(ks[7], (DEPTH, ML_HEADS), f32)
    ml_gate_bias = jnp.concatenate([i_bias, f_bias], axis=-1)
    ml_norm = 1.0 + nrm(ks[8], (DEPTH, ML_WIDTH), 0.02)
    ssd_conv_w = nrm(ks[9], (DEPTH, CONV_K, SSD_CONV_DIM), CONV_K ** -0.5)
    ssd_conv_b = nrm(ks[10], (DEPTH, SSD_CONV_DIM), 0.02)
    dt0 = jnp.exp(jax.random.uniform(ks[11], (DEPTH, SSD_HEADS), f32)
                  * (math.log(DT_MAX) - math.log(DT_MIN)) + math.log(DT_MIN))
    ssd_dt_bias = dt0 + jnp.log(-jnp.expm1(-dt0))
    ssd_a_log = jnp.log(jax.random.uniform(ks[12], (DEPTH, SSD_HEADS), f32, minval=1.0, maxval=16.0))
    ssd_d = 1.0 + nrm(ks[13], (DEPTH, SSD_HEADS), 0.1)
    ssd_norm = 1.0 + nrm(ks[14], (DEPTH, SSD_WIDTH), 0.02)
    ret_norm = 1.0 + nrm(ks[15], (DEPTH, RET_WIDTH), 0.02)
    w_out = nrm(ks[16], (DEPTH, MIX_WIDTH, D_MODEL), MIX_WIDTH ** -0.5 * resid_scale)
    norm_ffn = 1.0 + nrm(ks[17], (DEPTH, D_MODEL), 0.02)
    w_gate_up = nrm(ks[18], (DEPTH, D_MODEL, 2 * FFN_HIDDEN), D_MODEL ** -0.5)
    w_down = nrm(ks[19], (DEPTH, FFN_HIDDEN, D_MODEL), FFN_HIDDEN ** -0.5 * resid_scale)
    norm_final = 1.0 + nrm(ks[20], (D_MODEL,), 0.02)
    return {"x": x, "positions": positions, "norm_mix": norm_mix, "w_in": w_in,
            "ml_conv_w": ml_conv_w, "ml_conv_b": ml_conv_b, "ml_gate_bias": ml_gate_bias,
            "ml_norm": ml_norm, "ssd_conv_w": ssd_conv_w, "ssd_conv_b": ssd_conv_b,
            "ssd_dt_bias": ssd_dt_bias, "ssd_a_log": ssd_a_log, "ssd_d": ssd_d,
            "ssd_norm": ssd_norm, "ret_norm": ret_norm, "w_out": w_out, "norm_ffn": norm_ffn,
            "w_gate_up": w_gate_up, "w_down": w_down, "norm_final": norm_final}


def reference(x, positions, norm_mix, w_in, ml_conv_w, ml_conv_b, ml_gate_bias, ml_norm,
              ssd_conv_w, ssd_conv_b, ssd_dt_bias, ssd_a_log, ssd_d, ssd_norm, ret_norm,
              w_out, norm_ffn, w_gate_up, w_down, norm_final):
    cos, sin = rope_tables(positions)
    log_gamma = jnp.log1p(-jnp.exp2(-5.0 - jnp.arange(RET_HEADS, dtype=jnp.float32)))
    for l in range(DEPTH):
        x = hybrid_layer(x, cos, sin, log_gamma, norm_mix[l], w_in[l], ml_conv_w[l], ml_conv_b[l],
                         ml_gate_bias[l], ml_norm[l], ssd_conv_w[l], ssd_conv_b[l],
                         ssd_dt_bias[l], ssd_a_log[l], ssd_d[l], ssd_norm[l], ret_norm[l],
                         w_out[l], norm_ffn[l], w_gate_up[l], w_down[l])
    return rmsnorm(x, norm_final)
```

```python
import functools
import math

import jax
import jax.numpy as jnp
from jax import lax
from jax.experimental import pallas as pl
from jax.experimental.pallas import tpu as pltpu

F32 = jnp.float32
BF16 = jnp.bfloat16

D_MODEL = 1024
CHUNK = 128
CONV_K = 4
NORM_EPS = 1e-6
ML_HEADS = 6
ML_HEAD_DIM = 128
ML_WIDTH = ML_HEADS * ML_HEAD_DIM
SSD_HEADS = 12
SSD_HEAD_DIM = 64
SSD_WIDTH = SSD_HEADS * SSD_HEAD_DIM
SSD_GROUPS = 2
SSD_HEADS_PER_GROUP = SSD_HEADS // SSD_GROUPS
SSD_GROUP_WIDTH = SSD_HEADS_PER_GROUP * SSD_HEAD_DIM
SSD_STATE = 128
SSD_CONV_DIM = SSD_WIDTH + 2 * SSD_GROUPS * SSD_STATE
RET_HEADS = 4
RET_HEAD_DIM = 128
RET_WIDTH = RET_HEADS * RET_HEAD_DIM
ROPE_BASE = 10000.0
MIX_WIDTH = ML_WIDTH + SSD_WIDTH + RET_WIDTH
ML_IN = 4 * ML_WIDTH + 2 * ML_HEADS
SSD_IN = SSD_WIDTH + SSD_CONV_DIM + SSD_HEADS
RET_IN = 4 * RET_WIDTH
FFN_HIDDEN = 2816

WIDE_WIDTH = 4 * ML_WIDTH + SSD_WIDTH + SSD_CONV_DIM + 4 * RET_WIDTH
GATE_LANES = 128
GATE_ROWS = 32
I_LANE = 0
F_LANE = ML_HEADS
DT_LANE = 2 * ML_HEADS
TAIL = 8

VMEM_LIMIT = 56 * 1024 * 1024

_NT = (((1,), (1,)), ((), ()))
_TN = (((0,), (0,)), ((), ()))


def _sigmoid(x):
    return 1.0 / (1.0 + jnp.exp(-x))


def _silu(x):
    return x * _sigmoid(x)


def _log_sigmoid(x):
    return jnp.minimum(x, 0.0) - jnp.log1p(jnp.exp(-jnp.abs(x)))


def _softplus(x):
    return jnp.maximum(x, 0.0) + jnp.log1p(jnp.exp(-jnp.abs(x)))


def _dot(a, b):
    return jnp.dot(a, b, preferred_element_type=F32)


def _dot_exact(a, b):
    return jnp.dot(a, b, preferred_element_type=F32, precision=lax.Precision.HIGHEST)


def _tri_masks():
    row = lax.broadcasted_iota(jnp.int32, (CHUNK, CHUNK), 0)
    col = lax.broadcasted_iota(jnp.int32, (CHUNK, CHUNK), 1)
    return row, col


def _rope_kernel(pos_ref, invf_ref, cc_ref, ss_ref):
    ang = pos_ref[...].astype(F32) * invf_ref[...]
    lane = lax.broadcasted_iota(jnp.int32, ang.shape, 1)
    sin = jnp.sin(ang)
    cc_ref[...] = jnp.cos(ang)
    ss_ref[...] = jnp.where(lane < RET_HEAD_DIM // 2, -sin, sin)


def _rope_tables(positions):
    n = positions.size
    tm = min(n, 2048)
    assert n % tm == 0
    half = RET_HEAD_DIM // 2
    inv_freq = ROPE_BASE ** (-jnp.arange(0, RET_HEAD_DIM, 2, dtype=F32) / RET_HEAD_DIM)
    invf = jnp.concatenate([inv_freq, inv_freq]).reshape(1, 2 * half)
    return pl.pallas_call(
        _rope_kernel,
        grid=(n // tm,),
        in_specs=[pl.BlockSpec((tm, 1), lambda i: (i, 0)),
                  pl.BlockSpec((1, RET_HEAD_DIM), lambda i: (0, 0))],
        out_specs=[pl.BlockSpec((tm, RET_HEAD_DIM), lambda i: (i, 0))] * 2,
        out_shape=[jax.ShapeDtypeStruct((n, RET_HEAD_DIM), F32)] * 2,
        name="rope_tables",
    )(positions.reshape(n, 1), invf)


def _inproj_kernel(x_ref, g_ref, w_ref, wn_ref, wnt_ref, o_ref, gates_ref, gatest_ref, h_ref):
    @pl.when(pl.program_id(1) == 0)
    def _():
        x = x_ref[...]
        y = x * lax.rsqrt(jnp.mean(x * x, axis=-1, keepdims=True) + NORM_EPS) * g_ref[...]
        hb = y.astype(BF16)
        h_ref[...] = hb
        gates_ref[...] = _dot(hb, wn_ref[...])
        gatest_ref[...] = lax.dot_general(wnt_ref[...], hb, _NT, preferred_element_type=F32)

    o_ref[...] = _dot(h_ref[...], w_ref[...]).astype(BF16)


def _inproj(x, g, w_wide, w_nar, w_nar_t):
    n = x.shape[0]
    tm = min(n, 1024)
    tn = 1024
    assert n % tm == 0
    return pl.pallas_call(
        _inproj_kernel,
        grid=(n // tm, WIDE_WIDTH // tn),
        in_specs=[pl.BlockSpec((tm, D_MODEL), lambda i, j: (i, 0)),
                  pl.BlockSpec((1, D_MODEL), lambda i, j: (0, 0)),
                  pl.BlockSpec((D_MODEL, tn), lambda i, j: (0, j)),
                  pl.BlockSpec((D_MODEL, GATE_LANES), lambda i, j: (0, 0)),
                  pl.BlockSpec((GATE_ROWS, D_MODEL), lambda i, j: (0, 0))],
        out_specs=[pl.BlockSpec((tm, tn), lambda i, j: (i, j)),
                   pl.BlockSpec((tm, GATE_LANES), lambda i, j: (i, 0)),
                   pl.BlockSpec((GATE_ROWS, tm), lambda i, j: (0, i))],
        out_shape=[jax.ShapeDtypeStruct((n, WIDE_WIDTH), BF16),
                   jax.ShapeDtypeStruct((n, GATE_LANES), F32),
                   jax.ShapeDtypeStruct((GATE_ROWS, n), F32)],
        scratch_shapes=[pltpu.VMEM((tm, D_MODEL), BF16)],
        compiler_params=pltpu.CompilerParams(
            dimension_semantics=("parallel", "arbitrary"), vmem_limit_bytes=VMEM_LIMIT),
        name="inproj",
    )(x, g, w_wide, w_nar, w_nar_t)


def _conv_silu(xbuf, x_new, w_ref, b_ref, first):
    @pl.when(first)
    def _():
        xbuf[0:TAIL, :] = jnp.zeros((TAIL, xbuf.shape[1]), F32)

    xbuf[TAIL:TAIL + CHUNK, :] = x_new
    y = b_ref[...] + w_ref[CONV_K - 1:CONV_K, :] * x_new
    for j in range(1, CONV_K):
        y = y + w_ref[CONV_K - 1 - j:CONV_K - j, :] * xbuf[TAIL - j:TAIL - j + CHUNK, :]
    xbuf[0:TAIL, :] = xbuf[CHUNK:CHUNK + TAIL, :]
    return _silu(y)


def _mlstm_kernel(qk_ref, v_ref, o_ref, gates_ref, gatest_ref, brow_ref, bcol_ref,
                  cw_ref, cb_ref, norm_ref, y_ref, xbuf, c_ref, n_ref, m_ref):
    first = pl.program_id(1) == 0

    @pl.when(first)
    def _():
        c_ref[...] = jnp.zeros(c_ref.shape, F32)
        n_ref[...] = jnp.zeros(n_ref.shape, F32)
        m_ref[...] = jnp.zeros(m_ref.shape, F32)

    qk = _conv_silu(xbuf, qk_ref[...].astype(F32), cw_ref, cb_ref, first)

    row, col = _tri_masks()
    causal = col <= row
    tri = causal.astype(F32)
    tri_t = (row <= col).astype(F32)

    g_col = gates_ref[...] + brow_ref[...]
    g_row = gatest_ref[...] + bcol_ref[...]
    b_col = _dot_exact(tri, _log_sigmoid(g_col))
    b_row = _dot_exact(_log_sigmoid(g_row), tri_t)

    scale = ML_HEAD_DIM ** -0.5
    for h in range(ML_HEADS):
        sl = slice(h * ML_HEAD_DIM, (h + 1) * ML_HEAD_DIM)
        q = qk[:, sl]
        k = qk[:, ML_WIDTH + h * ML_HEAD_DIM:ML_WIDTH + (h + 1) * ML_HEAD_DIM] * scale
        v = v_ref[:, sl]
        qb = q.astype(BF16)
        kb = k.astype(BF16)

        bt = b_col[:, F_LANE + h:F_LANE + h + 1]
        bs = b_row[F_LANE + h:F_LANE + h + 1, :]
        i_s = g_row[I_LANE + h:I_LANE + h + 1, :]
        i_t = g_col[:, I_LANE + h:I_LANE + h + 1]
        m_prev = m_ref[h:h + 1, 0:1]

        log_d = jnp.where(causal, bt - bs + i_s, -jnp.inf)
        a = bt + m_prev
        m_t = jnp.maximum(a, jnp.max(log_d, axis=1, keepdims=True))
        decay = jnp.exp(log_d - m_t)
        inter = jnp.exp(a - m_t)
        scores = lax.dot_general(qb, kb, _NT, preferred_element_type=F32) * decay
        num = _dot(scores.astype(BF16), v) + inter * _dot(qb, c_ref[h].astype(BF16))
        den = (jnp.sum(scores, axis=1, keepdims=True)
               + inter * jnp.sum(q * n_ref[h:h + 1, :], axis=1, keepdims=True))
        out = num / jnp.maximum(jnp.abs(den), jnp.exp(-m_t))

        b_last = bt[CHUNK - 1:CHUNK, :]
        w_state = b_last - bt + i_t
        m_loc = jnp.max(w_state, axis=0, keepdims=True)
        m_new = jnp.maximum(b_last + m_prev, m_loc)
        a_old = jnp.exp(b_last + m_prev - m_new)
        ke = k * jnp.exp(w_state - m_new)
        c_ref[h] = a_old * c_ref[h] + lax.dot_general(ke.astype(BF16), v, _TN,
                                                      preferred_element_type=F32)
        n_ref[h:h + 1, :] = a_old * n_ref[h:h + 1, :] + jnp.sum(ke, axis=0, keepdims=True)
        m_ref[h:h + 1, :] = jnp.broadcast_to(m_new, (1, m_ref.shape[1]))

        yn = out * lax.rsqrt(jnp.mean(out * out, axis=1, keepdims=True) + NORM_EPS) * norm_ref[:, sl]
        y_ref[:, sl] = (yn * _sigmoid(o_ref[:, sl].astype(F32))).astype(BF16)


def _mlstm(proj, gates, gates_t, bias_row, bias_col, conv_w, conv_b, norm, bsz, nc):
    n = proj.shape[0]
    rows = lambda b, c: b * nc + c
    const = lambda b, c: (0, 0)
    return pl.pallas_call(
        _mlstm_kernel,
        grid=(bsz, nc),
        in_specs=[pl.BlockSpec((CHUNK, 2 * ML_WIDTH), lambda b, c: (rows(b, c), 0)),
                  pl.BlockSpec((CHUNK, ML_WIDTH), lambda b, c: (rows(b, c), 2)),
                  pl.BlockSpec((CHUNK, ML_WIDTH), lambda b, c: (rows(b, c), 3)),
                  pl.BlockSpec((CHUNK, GATE_LANES), lambda b, c: (rows(b, c), 0)),
                  pl.BlockSpec((GATE_ROWS, CHUNK), lambda b, c: (0, rows(b, c))),
                  pl.BlockSpec((1, GATE_LANES), const),
                  pl.BlockSpec((GATE_ROWS, 1), const),
                  pl.BlockSpec((CONV_K, 2 * ML_WIDTH), const),
                  pl.BlockSpec((1, 2 * ML_WIDTH), const),
                  pl.BlockSpec((1, ML_WIDTH), const)],
        out_specs=pl.BlockSpec((CHUNK, ML_WIDTH), lambda b, c: (rows(b, c), 0)),
        out_shape=jax.ShapeDtypeStruct((n, ML_WIDTH), BF16),
        scratch_shapes=[pltpu.VMEM((TAIL + CHUNK, 2 * ML_WIDTH), F32),
                        pltpu.VMEM((ML_HEADS, ML_HEAD_DIM, ML_HEAD_DIM), F32),
                        pltpu.VMEM((8, ML_HEAD_DIM), F32),
                        pltpu.VMEM((8, 128), F32)],
        compiler_params=pltpu.CompilerParams(
            dimension_semantics=("parallel", "arbitrary"), vmem_limit_bytes=VMEM_LIMIT),
        name="mlstm",
    )(proj, proj, proj, gates, gates_t, bias_row, bias_col, conv_w, conv_b, norm)


def _ssd_kernel(z_ref, xbc_ref, gates_ref, gatest_ref, brow_ref, bcol_ref, alog_row_ref,
                alog_col_ref, cw_ref, cb_ref, d_ref, norm_ref, y_ref, xbuf, st_ref, ybuf):
    first = pl.program_id(1) == 0

    @pl.when(first)
    def _():
        st_ref[...] = jnp.zeros(st_ref.shape, F32)

    xbc = _conv_silu(xbuf, xbc_ref[...].astype(F32), cw_ref, cb_ref, first)

    row, col = _tri_masks()
    causal = col <= row
    tri = causal.astype(F32)
    tri_t = (row <= col).astype(F32)

    dt_col = _softplus(gates_ref[...] + brow_ref[...])
    dt_row = _softplus(gatest_ref[...] + bcol_ref[...])
    a_col = dt_col * (-jnp.exp(alog_row_ref[...]))
    a_row = dt_row * (-jnp.exp(alog_col_ref[...]))
    acum_col = _dot_exact(tri, a_col)
    acum_row = _dot_exact(a_row, tri_t)

    for g in range(SSD_GROUPS):
        b_g = xbc[:, SSD_WIDTH + g * SSD_STATE:SSD_WIDTH + (g + 1) * SSD_STATE].astype(BF16)
        c_off = SSD_WIDTH + SSD_GROUPS * SSD_STATE
        c_g = xbc[:, c_off + g * SSD_STATE:c_off + (g + 1) * SSD_STATE].astype(BF16)
        cb = lax.dot_general(c_g, b_g, _NT, preferred_element_type=F32)
        prev_t = st_ref[g]
        y_off = _dot(c_g, prev_t.astype(BF16))

        xcd_parts = []
        cd_parts = []
        for hh in range(SSD_HEADS_PER_GROUP):
            h = g * SSD_HEADS_PER_GROUP + hh
            lane = DT_LANE + h
            sl = slice(h * SSD_HEAD_DIM, (h + 1) * SSD_HEAD_DIM)
            a_t = acum_col[:, lane:lane + 1]
            a_s = acum_row[lane:lane + 1, :]
            l_dec = jnp.exp(jnp.where(causal, a_t - a_s, -jnp.inf))
            xh = xbc[:, sl]
            xc = xh * dt_col[:, lane:lane + 1]
            y_diag = _dot((cb * l_dec).astype(BF16), xc.astype(BF16))
            y_h = (y_diag + y_off[:, hh * SSD_HEAD_DIM:(hh + 1) * SSD_HEAD_DIM] * jnp.exp(a_t)
                   + d_ref[:, sl] * xh)
            ybuf[:, sl] = y_h
            a_last = a_t[CHUNK - 1:CHUNK, :]
            xcd_parts.append((xc * jnp.exp(a_last - a_t)).astype(BF16))
            cd_parts.append(jnp.broadcast_to(jnp.exp(a_last), (1, SSD_HEAD_DIM)))
        xcd = jnp.concatenate(xcd_parts, axis=1)
        chunk_decay = jnp.concatenate(cd_parts, axis=1)
        st_ref[g] = chunk_decay * prev_t + lax.dot_general(b_g, xcd, _TN,
                                                           preferred_element_type=F32)

    y = ybuf[...] * _silu(z_ref[...].astype(F32))
    for g in range(SSD_GROUPS):
        sl = slice(g * SSD_GROUP_WIDTH, (g + 1) * SSD_GROUP_WIDTH)
        yg = y[:, sl]
        yn = yg * lax.rsqrt(jnp.mean(yg * yg, axis=1, keepdims=True) + NORM_EPS) * norm_ref[:, sl]
        y_ref[:, sl] = yn.astype(BF16)


def _ssd(proj, gates, gates_t, bias_row, bias_col, alog_row, alog_col, conv_w, conv_b, d_exp,
         norm, bsz, nc):
    n = proj.shape[0]
    rows = lambda b, c: b * nc + c
    const = lambda b, c: (0, 0)
    z_blk = (4 * ML_WIDTH) // SSD_WIDTH
    xbc_blk = (4 * ML_WIDTH + SSD_WIDTH) // SSD_CONV_DIM
    return pl.pallas_call(
        _ssd_kernel,
        grid=(bsz, nc),
        in_specs=[pl.BlockSpec((CHUNK, SSD_WIDTH), lambda b, c: (rows(b, c), z_blk)),
                  pl.BlockSpec((CHUNK, SSD_CONV_DIM), lambda b, c: (rows(b, c), xbc_blk)),
                  pl.BlockSpec((CHUNK, GATE_LANES), lambda b, c: (rows(b, c), 0)),
                  pl.BlockSpec((GATE_ROWS, CHUNK), lambda b, c: (0, rows(b, c))),
                  pl.BlockSpec((1, GATE_LANES), const),
                  pl.BlockSpec((GATE_ROWS, 1), const),
                  pl.BlockSpec((1, GATE_LANES), const),
                  pl.BlockSpec((GATE_ROWS, 1), const),
                  pl.BlockSpec((CONV_K, SSD_CONV_DIM), const),
                  pl.BlockSpec((1, SSD_CONV_DIM), const),
                  pl.BlockSpec((1, SSD_WIDTH), const),
                  pl.BlockSpec((1, SSD_WIDTH), const)],
        out_specs=pl.BlockSpec((CHUNK, SSD_WIDTH), lambda b, c: (rows(b, c), 0)),
        out_shape=jax.ShapeDtypeStruct((n, SSD_WIDTH), BF16),
        scratch_shapes=[pltpu.VMEM((TAIL + CHUNK, SSD_CONV_DIM), F32),
                        pltpu.VMEM((SSD_GROUPS, SSD_STATE, SSD_GROUP_WIDTH), F32),
                        pltpu.VMEM((CHUNK, SSD_WIDTH), F32)],
        compiler_params=pltpu.CompilerParams(
            dimension_semantics=("parallel", "arbitrary"), vmem_limit_bytes=VMEM_LIMIT),
        name="ssd",
    )(proj, proj, gates, gates_t, bias_row, bias_col, alog_row, alog_col, conv_w, conv_b,
      d_exp, norm)


def _ret_kernel(q_ref, k_ref, v_ref, g_ref, cc_ref, ss_ref, norm_ref, y_ref, r_ref):
    @pl.when(pl.program_id(1) == 0)
    def _():
        r_ref[...] = jnp.zeros(r_ref.shape, F32)

    row, col = _tri_masks()
    causal = col <= row
    rel = (row - col).astype(F32)
    t_col = lax.broadcasted_iota(jnp.int32, (CHUNK, 1), 0).astype(F32)
    cc = cc_ref[...]
    ss = ss_ref[...]
    half = RET_HEAD_DIM // 2
    scale = RET_HEAD_DIM ** -0.5

    def rope(t):
        return t * cc + pltpu.roll(t, half, 1) * ss

    for h in range(RET_HEADS):
        sl = slice(h * RET_HEAD_DIM, (h + 1) * RET_HEAD_DIM)
        log_gamma = math.log1p(-2.0 ** (-5.0 - h))
        q = rope(q_ref[:, sl].astype(F32))
        k = rope(k_ref[:, sl].astype(F32)) * scale
        v = v_ref[:, sl]
        qb = q.astype(BF16)
        d_mat = jnp.exp(jnp.where(causal, log_gamma * rel, -jnp.inf))
        scores = lax.dot_general(qb, k.astype(BF16), _NT, preferred_element_type=F32) * d_mat
        xi = jnp.exp(log_gamma * (t_col + 1.0))
        zeta = jnp.exp(log_gamma * (CHUNK - 1.0 - t_col))
        r_prev = r_ref[h]
        y = _dot(scores.astype(BF16), v) + _dot(qb, r_prev.astype(BF16)) * xi
        r_ref[h] = math.exp(log_gamma * CHUNK) * r_prev + lax.dot_general(
            (k * zeta).astype(BF16), v, _TN, preferred_element_type=F32)

        yc = y - jnp.mean(y, axis=1, keepdims=True)
        yn = yc * lax.rsqrt(jnp.mean(yc * yc, axis=1, keepdims=True) + NORM_EPS) * norm_ref[:, sl]
        y_ref[:, sl] = (yn * _silu(g_ref[:, sl].astype(F32))).astype(BF16)


def _retention(proj, cc, ss, norm, bsz, nc):
    n = proj.shape[0]
    rows = lambda b, c: b * nc + c
    const = lambda b, c: (0, 0)
    base = (4 * ML_WIDTH + SSD_WIDTH + SSD_CONV_DIM) // RET_WIDTH
    return pl.pallas_call(
        _ret_kernel,
        grid=(bsz, nc),
        in_specs=[pl.BlockSpec((CHUNK, RET_WIDTH), lambda b, c: (rows(b, c), base)),
                  pl.BlockSpec((CHUNK, RET_WIDTH), lambda b, c: (rows(b, c), base + 1)),
                  pl.BlockSpec((CHUNK, RET_WIDTH), lambda b, c: (rows(b, c), base + 2)),
                  pl.BlockSpec((CHUNK, RET_WIDTH), lambda b, c: (rows(b, c), base + 3)),
                  pl.BlockSpec((CHUNK, RET_HEAD_DIM), lambda b, c: (rows(b, c), 0)),
                  pl.BlockSpec((CHUNK, RET_HEAD_DIM), lambda b, c: (rows(b, c), 0)),
                  pl.BlockSpec((1, RET_WIDTH), const)],
        out_specs=pl.BlockSpec((CHUNK, RET_WIDTH), lambda b, c: (rows(b, c), 0)),
        out_shape=jax.ShapeDtypeStruct((n, RET_WIDTH), BF16),
        scratch_shapes=[pltpu.VMEM((RET_HEADS, RET_HEAD_DIM, RET_HEAD_DIM), F32)],
        compiler_params=pltpu.CompilerParams(
            dimension_semantics=("parallel", "arbitrary"), vmem_limit_bytes=VMEM_LIMIT),
        name="retention",
    )(proj, proj, proj, proj, cc, ss, norm)


FFN_BLOCK = 256


def _out_ffn_kernel(x_ref, yml_ref, yssd_ref, yret_ref, wo_ref, g_ref, wgu_ref, wd_ref,
                    gfin_ref, o_ref, *, final_norm):
    mix = jnp.concatenate([yml_ref[...], yssd_ref[...], yret_ref[...]], axis=1)
    x1 = x_ref[...] + _dot(mix, wo_ref[...])
    h = (x1 * lax.rsqrt(jnp.mean(x1 * x1, axis=-1, keepdims=True) + NORM_EPS) * g_ref[...]).astype(BF16)
    acc = x1
    for blk in range(0, FFN_HIDDEN, FFN_BLOCK):
        gate = _dot(h, wgu_ref[:, blk:blk + FFN_BLOCK])
        up = _dot(h, wgu_ref[:, FFN_HIDDEN + blk:FFN_HIDDEN + blk + FFN_BLOCK])
        acc = acc + _dot((_silu(gate) * up).astype(BF16), wd_ref[blk:blk + FFN_BLOCK, :])
    if final_norm:
        acc = acc * lax.rsqrt(jnp.mean(acc * acc, axis=-1, keepdims=True) + NORM_EPS) * gfin_ref[...]
    o_ref[...] = acc


def _out_ffn(x, y_ml, y_ssd, y_ret, w_out, g, w_gu, w_down, g_final, final_norm):
    n = x.shape[0]
    tm = min(n, 512)
    assert n % tm == 0
    rows = lambda i: (i, 0)
    const = lambda i: (0, 0)
    resident = functools.partial(pl.BlockSpec, index_map=const, pipeline_mode=pl.Buffered(1))
    return pl.pallas_call(
        functools.partial(_out_ffn_kernel, final_norm=final_norm),
        grid=(n // tm,),
        in_specs=[pl.BlockSpec((tm, D_MODEL), rows),
                  pl.BlockSpec((tm, ML_WIDTH), rows),
                  pl.BlockSpec((tm, SSD_WIDTH), rows),
                  pl.BlockSpec((tm, RET_WIDTH), rows),
                  resident((MIX_WIDTH, D_MODEL)),
                  pl.BlockSpec((1, D_MODEL), const),
                  resident((D_MODEL, 2 * FFN_HIDDEN)),
                  resident((FFN_HIDDEN, D_MODEL)),
                  pl.BlockSpec((1, D_MODEL), const)],
        out_specs=pl.BlockSpec((tm, D_MODEL), rows),
        out_shape=jax.ShapeDtypeStruct((n, D_MODEL), F32),
        compiler_params=pltpu.CompilerParams(
            dimension_semantics=("parallel",), vmem_limit_bytes=VMEM_LIMIT),
        name="out_ffn",
    )(x, y_ml, y_ssd, y_ret, w_out, g, w_gu, w_down, g_final)


def _pad_lanes(v, offset):
    return jnp.zeros((1, GATE_LANES), F32).at[0, offset:offset + v.shape[0]].set(v)


def _pad_rows(v, offset):
    return jnp.zeros((GATE_ROWS, 1), F32).at[offset:offset + v.shape[0], 0].set(v)


def kernel(x, positions, norm_mix, w_in, ml_conv_w, ml_conv_b, ml_gate_bias, ml_norm,
           ssd_conv_w, ssd_conv_b, ssd_dt_bias, ssd_a_log, ssd_d, ssd_norm, ret_norm,
           w_out, norm_ffn, w_gate_up, w_down, norm_final):
    bsz, seq, _ = x.shape
    depth = w_in.shape[0]
    n = bsz * seq
    nc = seq // CHUNK
    assert seq % CHUNK == 0 and x.shape[2] == D_MODEL and w_in.shape[2] == ML_IN + SSD_IN + RET_IN

    ssd0 = ML_IN
    ret0 = ML_IN + SSD_IN
    w_wide = jnp.concatenate(
        [w_in[:, :, 0:4 * ML_WIDTH],
         w_in[:, :, ssd0:ssd0 + SSD_WIDTH + SSD_CONV_DIM],
         w_in[:, :, ret0:ret0 + RET_IN]], axis=-1).astype(BF16)
    w_gate_cols = jnp.concatenate(
        [w_in[:, :, 4 * ML_WIDTH:ML_IN], w_in[:, :, ret0 - SSD_HEADS:ret0]], axis=-1)
    n_gate = w_gate_cols.shape[-1]
    w_nar = jnp.pad(w_gate_cols, ((0, 0), (0, 0), (0, GATE_LANES - n_gate))).astype(BF16)
    w_nar_t = jnp.pad(jnp.swapaxes(w_gate_cols, 1, 2),
                      ((0, 0), (0, GATE_ROWS - n_gate), (0, 0))).astype(BF16)
    w_out_b = w_out.astype(BF16)
    w_gu_b = w_gate_up.astype(BF16)
    w_down_b = w_down.astype(BF16)

    cc, ss = _rope_tables(positions)
    xf = x.reshape(n, D_MODEL)
    for l in range(depth):
        gate_bias = jnp.concatenate([ml_gate_bias[l], ssd_dt_bias[l]])
        bias_row = _pad_lanes(gate_bias, 0)
        bias_col = _pad_rows(gate_bias, 0)
        alog_row = _pad_lanes(ssd_a_log[l], DT_LANE)
        alog_col = _pad_rows(ssd_a_log[l], DT_LANE)
        d_exp = jnp.repeat(ssd_d[l], SSD_HEAD_DIM).reshape(1, SSD_WIDTH)

        proj, gates, gates_t = _inproj(xf, norm_mix[l].reshape(1, D_MODEL), w_wide[l], w_nar[l],
                                       w_nar_t[l])
        y_ml = _mlstm(proj, gates, gates_t, bias_row, bias_col, ml_conv_w[l],
                      ml_conv_b[l].reshape(1, -1), ml_norm[l].reshape(1, -1), bsz, nc)
        y_ssd = _ssd(proj, gates, gates_t, bias_row, bias_col, alog_row, alog_col, ssd_conv_w[l],
                     ssd_conv_b[l].reshape(1, -1), d_exp, ssd_norm[l].reshape(1, -1), bsz, nc)
        y_ret = _retention(proj, cc, ss, ret_norm[l].reshape(1, -1), bsz, nc)
        xf = _out_ffn(xf, y_ml, y_ssd, y_ret, w_out_b[l], norm_ffn[l].reshape(1, D_MODEL),
                      w_gu_b[l], w_down_b[l], norm_final.reshape(1, D_MODEL),
                      final_norm=(l == depth - 1))
    return xf.reshape(bsz, seq, D_MODEL)
```

```python
import functools
import math

import jax
import jax.numpy as jnp
from jax import lax
from jax.experimental import pallas as pl
from jax.experimental.pallas import tpu as pltpu

F32 = jnp.float32
BF16 = jnp.bfloat16

D_MODEL = 1024
CHUNK = 128
CONV_K = 4
NORM_EPS = 1e-6
ML_HEADS = 6
ML_HEAD_DIM = 128
ML_WIDTH = ML_HEADS * ML_HEAD_DIM
SSD_HEADS = 12
SSD_HEAD_DIM = 64
SSD_WIDTH = SSD_HEADS * SSD_HEAD_DIM
SSD_GROUPS = 2
SSD_HEADS_PER_GROUP = SSD_HEADS // SSD_GROUPS
SSD_GROUP_WIDTH = SSD_HEADS_PER_GROUP * SSD_HEAD_DIM
SSD_STATE = 128
SSD_BC_WIDTH = SSD_GROUPS * SSD_STATE
SSD_CONV_DIM = SSD_WIDTH + 2 * SSD_BC_WIDTH
RET_HEADS = 4
RET_HEAD_DIM = 128
RET_WIDTH = RET_HEADS * RET_HEAD_DIM
ROPE_BASE = 10000.0
MIX_WIDTH = ML_WIDTH + SSD_WIDTH + RET_WIDTH
ML_IN = 4 * ML_WIDTH + 2 * ML_HEADS
SSD_IN = SSD_WIDTH + SSD_CONV_DIM + SSD_HEADS
RET_IN = 4 * RET_WIDTH
FFN_HIDDEN = 2816

COL_RET_Q = 0
COL_RET_K = COL_RET_Q + RET_WIDTH
COL_RET_G = COL_RET_K + RET_WIDTH
COL_ML_Q = COL_RET_G + RET_WIDTH
COL_ML_K = COL_ML_Q + ML_WIDTH
COL_SSD_X = COL_ML_K + ML_WIDTH
COL_ML_O = COL_SSD_X + SSD_WIDTH
COL_SSD_Z = COL_ML_O + ML_WIDTH
COL_SSD_B = COL_SSD_Z + SSD_WIDTH
COL_SSD_C = COL_SSD_B + SSD_BC_WIDTH
WIDE_WIDTH = COL_SSD_C + SSD_BC_WIDTH
PROJ_BLOCK = 256
SEGMENTS = (
    (COL_RET_Q, 2 * RET_WIDTH, "plain"),
    (COL_RET_G, RET_WIDTH, "silu"),
    (COL_ML_Q, ML_WIDTH, "conv"),
    (COL_ML_K, ML_WIDTH, "conv_scaled"),
    (COL_SSD_X, SSD_WIDTH, "conv"),
    (COL_ML_O, ML_WIDTH, "sigmoid"),
    (COL_SSD_Z, SSD_WIDTH, "silu"),
    (COL_SSD_B, 2 * SSD_BC_WIDTH, "conv"),
)
CONV_WIDTH = 2 * ML_WIDTH + SSD_CONV_DIM

GATE_LANES = 128
GATE_ROWS = 32
I_LANE = 0
F_LANE = ML_HEADS
DT_LANE = 2 * ML_HEADS
ML_GATE_ROWS = 16
DT_ROW0 = 8
TAIL = 8

VMEM_LIMIT = 56 * 1024 * 1024

_NT = (((1,), (1,)), ((), ()))


def _sigmoid(x):
    return 1.0 / (1.0 + jnp.exp(-x))


def _silu(x):
    return x * _sigmoid(x)


def _log_sigmoid(x):
    return jnp.minimum(x, 0.0) - jnp.log1p(jnp.exp(-jnp.abs(x)))


def _softplus(x):
    return jnp.maximum(x, 0.0) + jnp.log1p(jnp.exp(-jnp.abs(x)))


def _dot(a, b):
    return jnp.dot(a, b, preferred_element_type=F32)


def _dot_nt(a, b):
    return lax.dot_general(a, b, _NT, preferred_element_type=F32)


def _dot_exact(a, b):
    return jnp.dot(a, b, preferred_element_type=F32, precision=lax.Precision.HIGHEST)


def _row_vec_dot(vec, mat_nt=None, mat=None):
    rows = 16
    v = jnp.broadcast_to(vec, (rows, vec.shape[1]))
    hi = v.astype(BF16).astype(F32)
    r = lax.broadcasted_iota(jnp.int32, v.shape, 0)
    lhs = jnp.where(r == 0, hi, jnp.where(r == 1, v - hi, 0.0)).astype(BF16)
    out = _dot_nt(lhs, mat_nt) if mat_nt is not None else _dot(lhs, mat)
    return jnp.sum(out, axis=0, keepdims=True)


def _chunk_iotas():
    row = lax.broadcasted_iota(jnp.int32, (CHUNK, CHUNK), 0)
    col = lax.broadcasted_iota(jnp.int32, (CHUNK, CHUNK), 1)
    return row, col


def _rope_kernel(pos_ref, invf_ref, cc_ref, ss_ref):
    ang = pos_ref[...].astype(F32) * invf_ref[...]
    lane = lax.broadcasted_iota(jnp.int32, ang.shape, 1)
    sin = jnp.sin(ang)
    cc_ref[...] = jnp.cos(ang)
    ss_ref[...] = jnp.where(lane < RET_HEAD_DIM // 2, -sin, sin)


def _rope_tables(positions):
    n = positions.size
    tm = min(n, 2048)
    assert n % tm == 0
    half = RET_HEAD_DIM // 2
    inv_freq = ROPE_BASE ** (-jnp.arange(0, RET_HEAD_DIM, 2, dtype=F32) / RET_HEAD_DIM)
    invf = jnp.concatenate([inv_freq, inv_freq]).reshape(1, 2 * half)
    return pl.pallas_call(
        _rope_kernel,
        grid=(n // tm,),
        in_specs=[pl.BlockSpec((tm, 1), lambda i: (i, 0)),
                  pl.BlockSpec((1, RET_HEAD_DIM), lambda i: (0, 0))],
        out_specs=[pl.BlockSpec((tm, RET_HEAD_DIM), lambda i: (i, 0))] * 2,
        out_shape=[jax.ShapeDtypeStruct((n, RET_HEAD_DIM), F32)] * 2,
        name="rope_tables",
    )(positions.reshape(n, 1), invf)


def _inproj_kernel(x_ref, g_ref, w_ref, wvml_ref, wvret_ref, wn_ref, wnt_ref, cw_ref, cb_ref,
                   o_ref, vml_ref, vret_ref, gates_ref, gatest_ref, h_ref, cbuf, *, tiles_per_seq):
    tm = x_ref.shape[0]

    @pl.when(pl.program_id(0) % tiles_per_seq == 0)
    def _():
        cbuf[0:TAIL, :] = jnp.zeros((TAIL, cbuf.shape[1]), F32)

    x = x_ref[...]
    y = x * lax.rsqrt(jnp.mean(x * x, axis=-1, keepdims=True) + NORM_EPS) * g_ref[...]
    h_ref[...] = y.astype(BF16)
    gates_ref[...] = _dot(h_ref[...], wn_ref[...])
    gatest_ref[...] = _dot_nt(wnt_ref[...], h_ref[...])

    def conv_epilogue(acc, cols, cc, scaled):
        cbuf[TAIL:TAIL + tm, cc] = acc
        out = cb_ref[:, cc] + cw_ref[CONV_K - 1:CONV_K, cc] * acc
        for j in range(1, CONV_K):
            out = out + cw_ref[CONV_K - 1 - j:CONV_K - j, cc] * cbuf[TAIL - j:TAIL - j + tm, cc]
        cbuf[0:TAIL, cc] = cbuf[tm:tm + TAIL, cc]
        out = _silu(out)
        if scaled:
            out = out * (ML_HEAD_DIM ** -0.5)
        o_ref[:, cols] = out.astype(BF16)

    def plain_epilogue(acc, cols, act):
        o_ref[:, cols] = (acc if act is None else act(acc)).astype(BF16)

    def t_epilogue(acc, ref, rows):
        ref[rows, :] = acc.astype(BF16)

    heavy, light = [], []
    conv_col = 0
    for start, width, kind in SEGMENTS:
        for blk in range(start, start + width, PROJ_BLOCK):
            cols = slice(blk, blk + PROJ_BLOCK)
            mm = functools.partial(lambda c: _dot(h_ref[...], w_ref[:, c]), cols)
            if kind in ("conv", "conv_scaled"):
                cc = slice(conv_col, conv_col + PROJ_BLOCK)
                conv_col += PROJ_BLOCK
                heavy.append((mm, functools.partial(conv_epilogue, cols=cols, cc=cc,
                                                    scaled=(kind == "conv_scaled"))))
            else:
                act = {"silu": _silu, "sigmoid": _sigmoid, "plain": None}[kind]
                light.append((mm, functools.partial(plain_epilogue, cols=cols, act=act)))
    for w_t_ref, out_ref, width in ((wvml_ref, vml_ref, ML_WIDTH), (wvret_ref, vret_ref, RET_WIDTH)):
        for blk in range(0, width, PROJ_BLOCK):
            rows = slice(blk, blk + PROJ_BLOCK)
            mm = functools.partial(lambda r, w: _dot_nt(w[r, :], h_ref[...]), rows, w_t_ref)
            light.append((mm, functools.partial(t_epilogue, ref=out_ref, rows=rows)))

    jobs = []
    while heavy or light:
        if light:
            jobs.append(light.pop(0))
        if heavy:
            jobs.append(heavy.pop(0))
    acc = jobs[0][0]()
    for i, (_, epilogue) in enumerate(jobs):
        nxt = jobs[i + 1][0]() if i + 1 < len(jobs) else None
        epilogue(acc)
        acc = nxt


def _inproj(x, g, w_wide, w_vml_t, w_vret_t, w_nar, w_nar_t, conv_w, conv_b, seq):
    n = x.shape[0]
    tm = min(seq, 512)
    assert seq % tm == 0 and n % seq == 0
    rows = lambda i: (i, 0)
    cols = lambda i: (0, i)
    const = lambda i: (0, 0)
    resident = functools.partial(pl.BlockSpec, index_map=const, pipeline_mode=pl.Buffered(1))
    return pl.pallas_call(
        functools.partial(_inproj_kernel, tiles_per_seq=seq // tm),
        grid=(n // tm,),
        in_specs=[pl.BlockSpec((tm, D_MODEL), rows),
                  pl.BlockSpec((1, D_MODEL), const),
                  resident((D_MODEL, WIDE_WIDTH)),
                  resident((ML_WIDTH, D_MODEL)),
                  resident((RET_WIDTH, D_MODEL)),
                  resident((D_MODEL, GATE_LANES)),
                  resident((GATE_ROWS, D_MODEL)),
                  pl.BlockSpec((CONV_K, CONV_WIDTH), const),
                  pl.BlockSpec((1, CONV_WIDTH), const)],
        out_specs=[pl.BlockSpec((tm, WIDE_WIDTH), rows),
                   pl.BlockSpec((ML_WIDTH, tm), cols),
                   pl.BlockSpec((RET_WIDTH, tm), cols),
                   pl.BlockSpec((tm, GATE_LANES), rows),
                   pl.BlockSpec((GATE_ROWS, tm), cols)],
        out_shape=[jax.ShapeDtypeStruct((n, WIDE_WIDTH), BF16),
                   jax.ShapeDtypeStruct((ML_WIDTH, n), BF16),
                   jax.ShapeDtypeStruct((RET_WIDTH, n), BF16),
                   jax.ShapeDtypeStruct((n, GATE_LANES), F32),
                   jax.ShapeDtypeStruct((GATE_ROWS, n), F32)],
        scratch_shapes=[pltpu.VMEM((tm, D_MODEL), BF16),
                        pltpu.VMEM((TAIL + tm, CONV_WIDTH), F32)],
        compiler_params=pltpu.CompilerParams(
            dimension_semantics=("arbitrary",), vmem_limit_bytes=VMEM_LIMIT),
        name="inproj",
    )(x, g, w_wide, w_vml_t, w_vret_t, w_nar, w_nar_t, conv_w, conv_b)


def _mlstm_kernel(*refs, bsz):
    q_ref, k_ref, o_ref, gates_ref = refs[0:4]
    vt_refs = refs[4:4 + bsz]
    gatest_refs = refs[4 + bsz:4 + 2 * bsz]
    brow_ref, bcol_ref, norm_ref, y_ref, ct_ref, n_ref, m_ref = refs[4 + 2 * bsz:]

    @pl.when(pl.program_id(0) == 0)
    def _():
        ct_ref[...] = jnp.zeros(ct_ref.shape, F32)
        n_ref[...] = jnp.zeros(n_ref.shape, F32)
        m_ref[...] = jnp.zeros(m_ref.shape, F32)

    row, col = _chunk_iotas()
    causal_t = row <= col
    tri = (col <= row).astype(F32)
    tri_t = causal_t.astype(F32)

    gate_terms = []
    for b in range(bsz):
        g_col = gates_ref[b] + brow_ref[...]
        g_row = gatest_refs[b][0:ML_GATE_ROWS, :] + bcol_ref[0:ML_GATE_ROWS, :]
        b_col = _dot_exact(tri, _log_sigmoid(g_col))
        b_row = _dot_exact(_log_sigmoid(g_row), tri_t)
        gate_terms.append((g_row, b_row, g_col - pltpu.roll(b_col, GATE_LANES - F_LANE, 1)))

    for b in range(bsz):
        g_row, b_row, cs_all = gate_terms[b]
        for h in range(ML_HEADS):
            sl = slice(h * ML_HEAD_DIM, (h + 1) * ML_HEAD_DIM)
            st = b * ML_HEADS + h
            q = q_ref[b, :, sl]
            k = k_ref[b, :, sl]
            vt = vt_refs[b][sl, :]
            ct_prev = ct_ref[st]
            n_prev = n_ref[st]
            m_prev = m_ref[st][:, 0:1]
            b_t = b_row[F_LANE + h:F_LANE + h + 1, :]
            i_t = g_row[I_LANE + h:I_LANE + h + 1, :]
            cs = cs_all[:, h:h + 1]

            log_d = jnp.where(causal_t, b_t + cs, -jnp.inf)
            a = b_t + m_prev
            m_t = jnp.maximum(a, jnp.max(log_d, axis=0, keepdims=True))
            decay = jnp.exp(log_d - m_t)
            inter = jnp.exp(a - m_t)
            scores = _dot_nt(k, q) * decay
            num = _dot(vt, scores.astype(BF16)) + inter * _dot_nt(ct_prev.astype(BF16), q)
            den = jnp.sum(scores, axis=0, keepdims=True) + inter * _row_vec_dot(n_prev, mat_nt=q)
            out = num * (1.0 / jnp.maximum(jnp.abs(den), jnp.exp(-m_t)))
            out = out * lax.rsqrt(jnp.mean(out * out, axis=0, keepdims=True) + NORM_EPS)
            y_ref[b, :, sl] = (out.T * norm_ref[:, sl] * o_ref[b, :, sl].astype(F32)).astype(BF16)

            b_last = b_t[:, CHUNK - 1:CHUNK]
            w_state = b_last - b_t + i_t
            m_new = jnp.maximum(b_last + m_prev, jnp.max(w_state, axis=1, keepdims=True))
            a_old = jnp.exp(b_last + m_prev - m_new)
            e = jnp.exp(w_state - m_new)
            ct_ref[st] = a_old * ct_prev + _dot((vt.astype(F32) * e).astype(BF16), k)
            n_ref[st] = a_old * n_prev + _row_vec_dot(e, mat=k)
            m_ref[st] = jnp.broadcast_to(m_new, m_ref.shape[1:])


def _per_batch_specs(rows, bsz, nc):
    return [pl.BlockSpec((rows, CHUNK), functools.partial(lambda b, c: (0, b * nc + c), b))
            for b in range(bsz)]


def _mlstm(proj, v_t, gates, gates_t, bias_row, bias_col, norm, bsz, nc):
    seq = nc * CHUNK
    tok3 = lambda blk: (lambda c: (0, c, blk))
    const = lambda c: (0, 0)
    per_batch = lambda rows: _per_batch_specs(rows, bsz, nc)
    return pl.pallas_call(
        functools.partial(_mlstm_kernel, bsz=bsz),
        grid=(nc,),
        in_specs=[pl.BlockSpec((bsz, CHUNK, ML_WIDTH), tok3(COL_ML_Q // ML_WIDTH)),
                  pl.BlockSpec((bsz, CHUNK, ML_WIDTH), tok3(COL_ML_K // ML_WIDTH)),
                  pl.BlockSpec((bsz, CHUNK, ML_WIDTH), tok3(COL_ML_O // ML_WIDTH)),
                  pl.BlockSpec((bsz, CHUNK, GATE_LANES), tok3(0))]
                 + per_batch(ML_WIDTH) + per_batch(GATE_ROWS)
                 + [pl.BlockSpec((1, GATE_LANES), const),
                    pl.BlockSpec((GATE_ROWS, 1), const),
                    pl.BlockSpec((1, ML_WIDTH), const)],
        out_specs=pl.BlockSpec((bsz, CHUNK, ML_WIDTH), tok3(0)),
        out_shape=jax.ShapeDtypeStruct((bsz, seq, ML_WIDTH), BF16),
        scratch_shapes=[pltpu.VMEM((bsz * ML_HEADS, ML_HEAD_DIM, ML_HEAD_DIM), F32),
                        pltpu.VMEM((bsz * ML_HEADS, 1, ML_HEAD_DIM), F32),
                        pltpu.VMEM((bsz * ML_HEADS, 1, 128), F32)],
        compiler_params=pltpu.CompilerParams(
            dimension_semantics=("arbitrary",), vmem_limit_bytes=VMEM_LIMIT),
        name="mlstm",
    )(proj, proj, proj, gates, *([v_t] * bsz), *([gates_t] * bsz), bias_row, bias_col, norm)


def _ssd_kernel(*refs, bsz):
    d_ref, x_ref, b_ref, c_ref, z_ref, gates_ref = refs[0:6]
    gatest_refs = refs[6:6 + bsz]
    (brow_ref, bcol_ref, alog_row_ref, alog_col_ref, norm_ref, y_ref,
     st_ref, xt_buf, yt_buf) = refs[6 + bsz:]

    @pl.when(pl.program_id(0) == 0)
    def _():
        st_ref[...] = jnp.zeros(st_ref.shape, F32)

    row, col = _chunk_iotas()
    causal_t = row <= col
    tri = (col <= row).astype(F32)
    tri_t = causal_t.astype(F32)
    rows16 = slice(DT_ROW0, DT_ROW0 + 16)
    a_neg_row = -jnp.exp(alog_row_ref[...])
    a_neg_col = -jnp.exp(alog_col_ref[rows16, :])

    gate_terms = []
    group_terms = []
    for b in range(bsz):
        dt_col = _softplus(gates_ref[b] + brow_ref[...])
        dt_row = _softplus(gatest_refs[b][rows16, :] + bcol_ref[rows16, :])
        acum_col = _dot_exact(tri, dt_col * a_neg_row)
        acum_row = _dot_exact(dt_row * a_neg_col, tri_t)
        gate_terms.append((dt_row, acum_col, acum_row))
        for blk in range(0, SSD_WIDTH, 128):
            xt_buf[b, blk:blk + 128, :] = x_ref[b, :, blk:blk + 128].astype(F32).T
        for g in range(SSD_GROUPS):
            gs = slice(g * SSD_STATE, (g + 1) * SSD_STATE)
            b_g = b_ref[b, :, gs]
            c_g = c_ref[b, :, gs]
            prev = st_ref[b * SSD_GROUPS + g]
            group_terms.append((b_g, prev, _dot_nt(b_g, c_g),
                                _dot_nt(prev.astype(BF16), c_g)))

    for b in range(bsz):
        dt_row, acum_col, acum_row = gate_terms[b]
        for g in range(SSD_GROUPS):
            st = b * SSD_GROUPS + g
            b_g, prev, cb, y_off = group_terms[st]

            xcd_parts = []
            decays = []
            for hh in range(SSD_HEADS_PER_GROUP):
                h = g * SSD_HEADS_PER_GROUP + hh
                r = DT_LANE + h - DT_ROW0
                lane = DT_LANE + h
                hs = slice(h * SSD_HEAD_DIM, (h + 1) * SSD_HEAD_DIM)
                a_t = acum_row[r:r + 1, :]
                a_s = acum_col[:, lane:lane + 1]
                l_dec = jnp.exp(jnp.where(causal_t, a_t - a_s, -jnp.inf))
                x_h = xt_buf[b, hs, :]
                xc = x_h * dt_row[r:r + 1, :]
                y_diag = _dot(xc.astype(BF16), (cb * l_dec).astype(BF16))
                yt_buf[b, hs, :] = (y_diag
                                    + y_off[hh * SSD_HEAD_DIM:(hh + 1) * SSD_HEAD_DIM, :] * jnp.exp(a_t)
                                    + d_ref[h] * x_h)
                a_last = a_t[:, CHUNK - 1:CHUNK]
                xcd_parts.append((xc * jnp.exp(a_last - a_t)).astype(BF16))
                decays.append(jnp.exp(a_last))
            upd = _dot(jnp.concatenate(xcd_parts, axis=0), b_g)
            for hh in range(SSD_HEADS_PER_GROUP):
                ps = slice(hh * SSD_HEAD_DIM, (hh + 1) * SSD_HEAD_DIM)
                st_ref[st, ps, :] = decays[hh] * prev[ps, :] + upd[ps, :]

    for b in range(bsz):
        for g in range(SSD_GROUPS):
            parts = []
            for blk in range(g * SSD_GROUP_WIDTH, (g + 1) * SSD_GROUP_WIDTH, 128):
                parts.append(yt_buf[b, blk:blk + 128, :].T * z_ref[b, :, blk:blk + 128].astype(F32))
            ssq = sum(jnp.sum(p * p, axis=1, keepdims=True) for p in parts)
            inv = lax.rsqrt(ssq * (1.0 / SSD_GROUP_WIDTH) + NORM_EPS)
            for i, p in enumerate(parts):
                cs = slice(g * SSD_GROUP_WIDTH + i * 128, g * SSD_GROUP_WIDTH + (i + 1) * 128)
                y_ref[b, :, cs] = (p * inv * norm_ref[:, cs]).astype(BF16)


def _ssd(d, proj, gates, gates_t, bias_row, bias_col, alog_row, alog_col, norm, bsz, nc):
    tok3 = lambda blk: (lambda c: (0, c, blk))
    const = lambda c: (0, 0)
    return pl.pallas_call(
        functools.partial(_ssd_kernel, bsz=bsz),
        grid=(nc,),
        in_specs=[pl.BlockSpec(memory_space=pltpu.SMEM),
                  pl.BlockSpec((bsz, CHUNK, SSD_WIDTH), tok3(COL_SSD_X // SSD_WIDTH)),
                  pl.BlockSpec((bsz, CHUNK, SSD_BC_WIDTH), tok3(COL_SSD_B // SSD_BC_WIDTH)),
                  pl.BlockSpec((bsz, CHUNK, SSD_BC_WIDTH), tok3(COL_SSD_C // SSD_BC_WIDTH)),
                  pl.BlockSpec((bsz, CHUNK, SSD_WIDTH), tok3(COL_SSD_Z // SSD_WIDTH)),
                  pl.BlockSpec((bsz, CHUNK, GATE_LANES), tok3(0))]
                 + _per_batch_specs(GATE_ROWS, bsz, nc)
                 + [pl.BlockSpec((1, GATE_LANES), const),
                    pl.BlockSpec((GATE_ROWS, 1), const),
                    pl.BlockSpec((1, GATE_LANES), const),
                    pl.BlockSpec((GATE_ROWS, 1), const),
                    pl.BlockSpec((1, SSD_WIDTH), const)],
        out_specs=pl.BlockSpec((bsz, CHUNK, SSD_WIDTH), tok3(0)),
        out_shape=jax.ShapeDtypeStruct((bsz, nc * CHUNK, SSD_WIDTH), BF16),
        scratch_shapes=[pltpu.VMEM((bsz * SSD_GROUPS, SSD_GROUP_WIDTH, SSD_STATE), F32),
                        pltpu.VMEM((bsz, SSD_WIDTH, CHUNK), F32),
                        pltpu.VMEM((bsz, SSD_WIDTH, CHUNK), F32)],
        compiler_params=pltpu.CompilerParams(
            dimension_semantics=("arbitrary",), vmem_limit_bytes=VMEM_LIMIT),
        name="ssd",
    )(d, proj, proj, proj, proj, gates, *([gates_t] * bsz), bias_row, bias_col, alog_row, alog_col,
      norm)


def _ret_kernel(*refs, bsz):
    q_ref, k_ref, g_ref, cc_ref, ss_ref = refs[0:5]
    vt_refs = refs[5:5 + bsz]
    norm_ref, y_ref, rt_ref, dmat_ref = refs[5 + bsz:]
    log_gammas = [math.log1p(-2.0 ** (-5.0 - h)) for h in range(RET_HEADS)]

    @pl.when(pl.program_id(0) == 0)
    def _():
        rt_ref[...] = jnp.zeros(rt_ref.shape, F32)
        row, col = _chunk_iotas()
        rel = (col - row).astype(F32)
        for h in range(RET_HEADS):
            dmat_ref[h] = jnp.exp(jnp.where(row <= col, log_gammas[h] * rel, -jnp.inf))

    t_row = lax.broadcasted_iota(jnp.int32, (1, CHUNK), 1).astype(F32)
    half = RET_HEAD_DIM // 2
    scale = RET_HEAD_DIM ** -0.5

    roped = []
    for b in range(bsz):
        cc = cc_ref[b]
        ss = ss_ref[b]
        for h in range(RET_HEADS):
            sl = slice(h * RET_HEAD_DIM, (h + 1) * RET_HEAD_DIM)
            q = q_ref[b, :, sl].astype(F32)
            k = k_ref[b, :, sl].astype(F32)
            roped.append(((q * cc + pltpu.roll(q, half, 1) * ss).astype(BF16),
                          ((k * cc + pltpu.roll(k, half, 1) * ss) * scale).astype(BF16)))

    for b in range(bsz):
        for h in range(RET_HEADS):
            sl = slice(h * RET_HEAD_DIM, (h + 1) * RET_HEAD_DIM)
            st = b * RET_HEADS + h
            log_gamma = log_gammas[h]
            q, k = roped[st]
            vt = vt_refs[b][sl, :]
            xi = jnp.exp(log_gamma * (t_row + 1.0))
            zeta = jnp.exp(log_gamma * (CHUNK - 1.0 - t_row))
            rt_prev = rt_ref[st]
            scores = _dot_nt(k, q) * dmat_ref[h]
            y = _dot(vt, scores.astype(BF16)) + _dot_nt(rt_prev.astype(BF16), q) * xi
            rt_ref[st] = (math.exp(log_gamma * CHUNK) * rt_prev
                          + _dot((vt.astype(F32) * zeta).astype(BF16), k))

            yc = y - jnp.mean(y, axis=0, keepdims=True)
            yn = yc * lax.rsqrt(jnp.mean(yc * yc, axis=0, keepdims=True) + NORM_EPS)
            y_ref[b, :, sl] = (yn.T * norm_ref[:, sl] * g_ref[b, :, sl].astype(F32)).astype(BF16)


def _retention(proj, v_t, cc, ss, norm, bsz, nc):
    tok3 = lambda blk: (lambda c: (0, c, blk))
    const = lambda c: (0, 0)
    return pl.pallas_call(
        functools.partial(_ret_kernel, bsz=bsz),
        grid=(nc,),
        in_specs=[pl.BlockSpec((bsz, CHUNK, RET_WIDTH), tok3(COL_RET_Q // RET_WIDTH)),
                  pl.BlockSpec((bsz, CHUNK, RET_WIDTH), tok3(COL_RET_K // RET_WIDTH)),
                  pl.BlockSpec((bsz, CHUNK, RET_WIDTH), tok3(COL_RET_G // RET_WIDTH)),
                  pl.BlockSpec((bsz, CHUNK, RET_HEAD_DIM), tok3(0)),
                  pl.BlockSpec((bsz, CHUNK, RET_HEAD_DIM), tok3(0))]
                 + _per_batch_specs(RET_WIDTH, bsz, nc)
                 + [pl.BlockSpec((1, RET_WIDTH), const)],
        out_specs=pl.BlockSpec((bsz, CHUNK, RET_WIDTH), tok3(0)),
        out_shape=jax.ShapeDtypeStruct((bsz, nc * CHUNK, RET_WIDTH), BF16),
        scratch_shapes=[pltpu.VMEM((bsz * RET_HEADS, RET_HEAD_DIM, RET_HEAD_DIM), F32),
                        pltpu.VMEM((RET_HEADS, CHUNK, CHUNK), F32)],
        compiler_params=pltpu.CompilerParams(
            dimension_semantics=("arbitrary",), vmem_limit_bytes=VMEM_LIMIT),
        name="retention",
    )(proj, proj, proj, cc, ss, *([v_t] * bsz), norm)


FFN_BLOCK = 256


def _out_ffn_kernel(x_ref, yml_ref, yssd_ref, yret_ref, wo_ref, g_ref, wgu_ref, wd_ref,
                    gfin_ref, o_ref, *, final_norm):
    mix = jnp.concatenate([yml_ref[...], yssd_ref[...], yret_ref[...]], axis=1)
    x1 = x_ref[...] + _dot(mix, wo_ref[...])
    h = (x1 * lax.rsqrt(jnp.mean(x1 * x1, axis=-1, keepdims=True) + NORM_EPS) * g_ref[...]).astype(BF16)
    acc = x1
    for blk in range(0, FFN_HIDDEN, FFN_BLOCK):
        gate = _dot(h, wgu_ref[:, blk:blk + FFN_BLOCK])
        up = _dot(h, wgu_ref[:, FFN_HIDDEN + blk:FFN_HIDDEN + blk + FFN_BLOCK])
        acc = acc + _dot((_silu(gate) * up).astype(BF16), wd_ref[blk:blk + FFN_BLOCK, :])
    if final_norm:
        acc = acc * lax.rsqrt(jnp.mean(acc * acc, axis=-1, keepdims=True) + NORM_EPS) * gfin_ref[...]
    o_ref[...] = acc


def _out_ffn(x, y_ml, y_ssd, y_ret, w_out, g, w_gu, w_down, g_final, final_norm):
    n = x.shape[0]
    tm = min(n, 512)
    assert n % tm == 0
    rows = lambda i: (i, 0)
    const = lambda i: (0, 0)
    resident = functools.partial(pl.BlockSpec, index_map=const, pipeline_mode=pl.Buffered(1))
    return pl.pallas_call(
        functools.partial(_out_ffn_kernel, final_norm=final_norm),
        grid=(n // tm,),
        in_specs=[pl.BlockSpec((tm, D_MODEL), rows),
                  pl.BlockSpec((tm, ML_WIDTH), rows),
                  pl.BlockSpec((tm, SSD_WIDTH), rows),
                  pl.BlockSpec((tm, RET_WIDTH), rows),
                  resident((MIX_WIDTH, D_MODEL)),
                  pl.BlockSpec((1, D_MODEL), const),
                  resident((D_MODEL, 2 * FFN_HIDDEN)),
                  resident((FFN_HIDDEN, D_MODEL)),
                  pl.BlockSpec((1, D_MODEL), const)],
        out_specs=pl.BlockSpec((tm, D_MODEL), rows),
        out_shape=jax.ShapeDtypeStruct((n, D_MODEL), F32),
        compiler_params=pltpu.CompilerParams(
            dimension_semantics=("parallel",), vmem_limit_bytes=VMEM_LIMIT),
        name="out_ffn",
    )(x, y_ml, y_ssd, y_ret, w_out, g, w_gu, w_down, g_final)


def _gate_row(v, offset):
    depth, k = v.shape
    return jnp.zeros((depth, 1, GATE_LANES), F32).at[:, 0, offset:offset + k].set(v)


def _gate_col(v, offset):
    depth, k = v.shape
    return jnp.zeros((depth, GATE_ROWS, 1), F32).at[:, offset:offset + k, 0].set(v)


def kernel(x, positions, norm_mix, w_in, ml_conv_w, ml_conv_b, ml_gate_bias, ml_norm,
           ssd_conv_w, ssd_conv_b, ssd_dt_bias, ssd_a_log, ssd_d, ssd_norm, ret_norm,
           w_out, norm_ffn, w_gate_up, w_down, norm_final):
    bsz, seq, _ = x.shape
    depth = w_in.shape[0]
    n = bsz * seq
    nc = seq // CHUNK
    assert seq % CHUNK == 0 and x.shape[2] == D_MODEL and w_in.shape[2] == ML_IN + SSD_IN + RET_IN

    ml0, ssd0, ret0 = 0, ML_IN, ML_IN + SSD_IN
    seg = lambda start, width: w_in[:, :, start:start + width]
    ml_q, ml_k = seg(ml0, ML_WIDTH), seg(ml0 + ML_WIDTH, ML_WIDTH)
    ml_v, ml_o = seg(ml0 + 2 * ML_WIDTH, ML_WIDTH), seg(ml0 + 3 * ML_WIDTH, ML_WIDTH)
    ml_if = seg(ml0 + 4 * ML_WIDTH, 2 * ML_HEADS)
    ssd_z, ssd_x = seg(ssd0, SSD_WIDTH), seg(ssd0 + SSD_WIDTH, SSD_WIDTH)
    ssd_bc = seg(ssd0 + 2 * SSD_WIDTH, 2 * SSD_BC_WIDTH)
    ssd_dt = seg(ssd0 + SSD_WIDTH + SSD_CONV_DIM, SSD_HEADS)
    ret_qk, ret_v = seg(ret0, 2 * RET_WIDTH), seg(ret0 + 2 * RET_WIDTH, RET_WIDTH)
    ret_g = seg(ret0 + 3 * RET_WIDTH, RET_WIDTH)
    w_wide = jnp.concatenate([ret_qk, ret_g, ml_q, ml_k, ssd_x, ml_o, ssd_z, ssd_bc], axis=-1).astype(BF16)
    w_vml_t = jnp.swapaxes(ml_v, 1, 2).astype(BF16)
    w_vret_t = jnp.swapaxes(ret_v, 1, 2).astype(BF16)
    w_gate_cols = jnp.concatenate([ml_if, ssd_dt], axis=-1)
    n_gate = w_gate_cols.shape[-1]
    w_nar = jnp.pad(w_gate_cols, ((0, 0), (0, 0), (0, GATE_LANES - n_gate))).astype(BF16)
    w_nar_t = jnp.pad(jnp.swapaxes(w_gate_cols, 1, 2),
                      ((0, 0), (0, GATE_ROWS - n_gate), (0, 0))).astype(BF16)
    conv_w = jnp.concatenate([ml_conv_w, ssd_conv_w], axis=-1)
    conv_b = jnp.concatenate([ml_conv_b, ssd_conv_b], axis=-1)[:, None, :]
    w_out_b = w_out.astype(BF16)
    w_gu_b = w_gate_up.astype(BF16)
    w_down_b = w_down.astype(BF16)

    gate_bias = jnp.concatenate([ml_gate_bias, ssd_dt_bias], axis=-1)
    bias_row, bias_col = _gate_row(gate_bias, 0), _gate_col(gate_bias, 0)
    alog_row, alog_col = _gate_row(ssd_a_log, DT_LANE), _gate_col(ssd_a_log, DT_LANE)
    row3 = lambda v: v[:, None, :]
    norm_mix3, norm_ffn3 = row3(norm_mix), row3(norm_ffn)
    ml_norm3, ssd_norm3, ret_norm3 = row3(ml_norm), row3(ssd_norm), row3(ret_norm)
    g_final = norm_final.reshape(1, D_MODEL)

    cc, ss = _rope_tables(positions)
    cc = cc.reshape(bsz, seq, RET_HEAD_DIM)
    ss = ss.reshape(bsz, seq, RET_HEAD_DIM)
    xf = x.reshape(n, D_MODEL)
    for l in range(depth):
        proj, vml_t, vret_t, gates, gates_t = _inproj(
            xf, norm_mix3[l], w_wide[l], w_vml_t[l], w_vret_t[l], w_nar[l], w_nar_t[l],
            conv_w[l], conv_b[l], seq)
        proj = proj.reshape(bsz, seq, WIDE_WIDTH)
        gates = gates.reshape(bsz, seq, GATE_LANES)
        y_ml = _mlstm(proj, vml_t, gates, gates_t, bias_row[l], bias_col[l], ml_norm3[l], bsz, nc)
        y_ssd = _ssd(ssd_d[l], proj, gates, gates_t, bias_row[l], bias_col[l], alog_row[l],
                     alog_col[l], ssd_norm3[l], bsz, nc)
        y_ret = _retention(proj, vret_t, cc, ss, ret_norm3[l], bsz, nc)
        xf = _out_ffn(xf, y_ml.reshape(n, ML_WIDTH), y_ssd.reshape(n, SSD_WIDTH),
                      y_ret.reshape(n, RET_WIDTH), w_out_b[l], norm_ffn3[l], w_gu_b[l], w_down_b[l],
                      g_final, final_norm=(l == depth - 1))
    return xf.reshape(bsz, seq, D_MODEL)
```

```python
import functools
import math

import jax
import jax.numpy as jnp
from jax import lax
from jax.experimental import pallas as pl
from jax.experimental.pallas import tpu as pltpu

F32 = jnp.float32
BF16 = jnp.bfloat16

D_MODEL = 1024
CHUNK = 128
CONV_K = 4
NORM_EPS = 1e-6
ML_HEADS = 6
ML_HEAD_DIM = 128
ML_WIDTH = ML_HEADS * ML_HEAD_DIM
SSD_HEADS = 12
SSD_HEAD_DIM = 64
SSD_WIDTH = SSD_HEADS * SSD_HEAD_DIM
SSD_GROUPS = 2
SSD_HEADS_PER_GROUP = SSD_HEADS // SSD_GROUPS
SSD_GROUP_WIDTH = SSD_HEADS_PER_GROUP * SSD_HEAD_DIM
SSD_STATE = 128
SSD_BC_WIDTH = SSD_GROUPS * SSD_STATE
SSD_CONV_DIM = SSD_WIDTH + 2 * SSD_BC_WIDTH
RET_HEADS = 4
RET_HEAD_DIM = 128
RET_WIDTH = RET_HEADS * RET_HEAD_DIM
ROPE_BASE = 10000.0
MIX_WIDTH = ML_WIDTH + SSD_WIDTH + RET_WIDTH
ML_IN = 4 * ML_WIDTH + 2 * ML_HEADS
SSD_IN = SSD_WIDTH + SSD_CONV_DIM + SSD_HEADS
RET_IN = 4 * RET_WIDTH
FFN_HIDDEN = 2816

COL_RET_Q = 0
COL_RET_K = COL_RET_Q + RET_WIDTH
COL_RET_G = COL_RET_K + RET_WIDTH
COL_ML_Q = COL_RET_G + RET_WIDTH
COL_ML_K = COL_ML_Q + ML_WIDTH
COL_SSD_X = COL_ML_K + ML_WIDTH
COL_ML_O = COL_SSD_X + SSD_WIDTH
COL_SSD_Z = COL_ML_O + ML_WIDTH
COL_SSD_B = COL_SSD_Z + SSD_WIDTH
COL_SSD_C = COL_SSD_B + SSD_BC_WIDTH
WIDE_WIDTH = COL_SSD_C + SSD_BC_WIDTH
PROJ_BLOCK = 256
SEGMENTS = (
    (COL_RET_Q, 2 * RET_WIDTH, "plain"),
    (COL_RET_G, RET_WIDTH, "silu"),
    (COL_ML_Q, ML_WIDTH, "conv"),
    (COL_ML_K, ML_WIDTH, "conv_scaled"),
    (COL_SSD_X, SSD_WIDTH, "conv"),
    (COL_ML_O, ML_WIDTH, "sigmoid"),
    (COL_SSD_Z, SSD_WIDTH, "silu"),
    (COL_SSD_B, 2 * SSD_BC_WIDTH, "conv"),
)
CONV_WIDTH = 2 * ML_WIDTH + SSD_CONV_DIM

GATE_LANES = 128
GATE_ROWS = 32
I_LANE = 0
F_LANE = ML_HEADS
DT_LANE = 2 * ML_HEADS
ML_GATE_ROWS = 16
DT_ROW0 = 8
TAIL = 8

VMEM_LIMIT = 56 * 1024 * 1024

_NT = (((1,), (1,)), ((), ()))


def _sigmoid(x):
    return 1.0 / (1.0 + jnp.exp(-x))


def _silu(x):
    return x * _sigmoid(x)


def _log_sigmoid(x):
    return jnp.minimum(x, 0.0) - jnp.log1p(jnp.exp(-jnp.abs(x)))


def _softplus(x):
    return jnp.maximum(x, 0.0) + jnp.log1p(jnp.exp(-jnp.abs(x)))


def _dot(a, b):
    return jnp.dot(a, b, preferred_element_type=F32)


def _dot_nt(a, b):
    return lax.dot_general(a, b, _NT, preferred_element_type=F32)


def _dot_exact(a, b):
    return jnp.dot(a, b, preferred_element_type=F32, precision=lax.Precision.HIGHEST)


def _row_vec_dot(vec, mat_nt=None, mat=None):
    rows = 16
    v = jnp.broadcast_to(vec, (rows, vec.shape[1]))
    hi = v.astype(BF16).astype(F32)
    r = lax.broadcasted_iota(jnp.int32, v.shape, 0)
    lhs = jnp.where(r == 0, hi, jnp.where(r == 1, v - hi, 0.0)).astype(BF16)
    out = _dot_nt(lhs, mat_nt) if mat_nt is not None else _dot(lhs, mat)
    return jnp.sum(out, axis=0, keepdims=True)


def _chunk_iotas():
    row = lax.broadcasted_iota(jnp.int32, (CHUNK, CHUNK), 0)
    col = lax.broadcasted_iota(jnp.int32, (CHUNK, CHUNK), 1)
    return row, col


def _rope_kernel(pos_ref, invf_ref, cc_ref, ss_ref):
    ang = pos_ref[...].astype(F32) * invf_ref[...]
    lane = lax.broadcasted_iota(jnp.int32, ang.shape, 1)
    sin = jnp.sin(ang)
    cc_ref[...] = jnp.cos(ang)
    ss_ref[...] = jnp.where(lane < RET_HEAD_DIM // 2, -sin, sin)


def _rope_tables(positions):
    n = positions.size
    tm = min(n, 2048)
    assert n % tm == 0
    half = RET_HEAD_DIM // 2
    inv_freq = ROPE_BASE ** (-jnp.arange(0, RET_HEAD_DIM, 2, dtype=F32) / RET_HEAD_DIM)
    invf = jnp.concatenate([inv_freq, inv_freq]).reshape(1, 2 * half)
    return pl.pallas_call(
        _rope_kernel,
        grid=(n // tm,),
        in_specs=[pl.BlockSpec((tm, 1), lambda i: (i, 0)),
                  pl.BlockSpec((1, RET_HEAD_DIM), lambda i: (0, 0))],
        out_specs=[pl.BlockSpec((tm, RET_HEAD_DIM), lambda i: (i, 0))] * 2,
        out_shape=[jax.ShapeDtypeStruct((n, RET_HEAD_DIM), F32)] * 2,
        name="rope_tables",
    )(positions.reshape(n, 1), invf)


def _inproj_kernel(x_ref, g_ref, w_ref, wvml_ref, wvret_ref, wn_ref, wnt_ref, cw_ref, cb_ref,
                   o_ref, vml_ref, vret_ref, gates_ref, gatest_ref, h_ref, cbuf, *, tiles_per_seq):
    tm = x_ref.shape[0]

    @pl.when(pl.program_id(0) % tiles_per_seq == 0)
    def _():
        cbuf[0:TAIL, :] = jnp.zeros((TAIL, cbuf.shape[1]), F32)

    x = x_ref[...]
    y = x * lax.rsqrt(jnp.mean(x * x, axis=-1, keepdims=True) + NORM_EPS) * g_ref[...]
    h_ref[...] = y.astype(BF16)
    gates_ref[...] = _dot(h_ref[...], wn_ref[...])
    gatest_ref[...] = _dot_nt(wnt_ref[...], h_ref[...])

    def conv_epilogue(acc, cols, cc, scaled):
        cbuf[TAIL:TAIL + tm, cc] = acc
        out = cb_ref[:, cc] + cw_ref[CONV_K - 1:CONV_K, cc] * acc
        for j in range(1, CONV_K):
            out = out + cw_ref[CONV_K - 1 - j:CONV_K - j, cc] * cbuf[TAIL - j:TAIL - j + tm, cc]
        cbuf[0:TAIL, cc] = cbuf[tm:tm + TAIL, cc]
        out = _silu(out)
        if scaled:
            out = out * (ML_HEAD_DIM ** -0.5)
        o_ref[:, cols] = out.astype(BF16)

    def plain_epilogue(acc, cols, act):
        o_ref[:, cols] = (acc if act is None else act(acc)).astype(BF16)

    def t_epilogue(acc, ref, rows):
        ref[rows, :] = acc.astype(BF16)

    heavy, light = [], []
    conv_col = 0
    for start, width, kind in SEGMENTS:
        for blk in range(start, start + width, PROJ_BLOCK):
            cols = slice(blk, blk + PROJ_BLOCK)
            mm = functools.partial(lambda c: _dot(h_ref[...], w_ref[:, c]), cols)
            if kind in ("conv", "conv_scaled"):
                cc = slice(conv_col, conv_col + PROJ_BLOCK)
                conv_col += PROJ_BLOCK
                heavy.append((mm, functools.partial(conv_epilogue, cols=cols, cc=cc,
                                                    scaled=(kind == "conv_scaled"))))
            else:
                act = {"silu": _silu, "sigmoid": _sigmoid, "plain": None}[kind]
                light.append((mm, functools.partial(plain_epilogue, cols=cols, act=act)))
    for w_t_ref, out_ref, width in ((wvml_ref, vml_ref, ML_WIDTH), (wvret_ref, vret_ref, RET_WIDTH)):
        for blk in range(0, width, PROJ_BLOCK):
            rows = slice(blk, blk + PROJ_BLOCK)
            mm = functools.partial(lambda r, w: _dot_nt(w[r, :], h_ref[...]), rows, w_t_ref)
            light.append((mm, functools.partial(t_epilogue, ref=out_ref, rows=rows)))

    jobs = []
    while heavy or light:
        if light:
            jobs.append(light.pop(0))
        if heavy:
            jobs.append(heavy.pop(0))
    acc = jobs[0][0]()
    for i, (_, epilogue) in enumerate(jobs):
        nxt = jobs[i + 1][0]() if i + 1 < len(jobs) else None
        epilogue(acc)
        acc = nxt


def _layer_spec(layer, shape, **kwargs):
    return pl.BlockSpec((None,) + tuple(shape), lambda *_: (layer,) + (0,) * len(shape), **kwargs)


def _inproj(layer, x, g, w_wide, w_vml_t, w_vret_t, w_nar, w_nar_t, conv_w, conv_b, seq):
    n = x.shape[0]
    tm = min(seq, 512)
    assert seq % tm == 0 and n % seq == 0
    rows = lambda i: (i, 0)
    cols = lambda i: (0, i)
    resident = functools.partial(_layer_spec, layer, pipeline_mode=pl.Buffered(1))
    return pl.pallas_call(
        functools.partial(_inproj_kernel, tiles_per_seq=seq // tm),
        grid=(n // tm,),
        in_specs=[pl.BlockSpec((tm, D_MODEL), rows),
                  _layer_spec(layer, (1, D_MODEL)),
                  resident((D_MODEL, WIDE_WIDTH)),
                  resident((ML_WIDTH, D_MODEL)),
                  resident((RET_WIDTH, D_MODEL)),
                  resident((D_MODEL, GATE_LANES)),
                  resident((GATE_ROWS, D_MODEL)),
                  _layer_spec(layer, (CONV_K, CONV_WIDTH)),
                  _layer_spec(layer, (1, CONV_WIDTH))],
        out_specs=[pl.BlockSpec((tm, WIDE_WIDTH), rows),
                   pl.BlockSpec((ML_WIDTH, tm), cols),
                   pl.BlockSpec((RET_WIDTH, tm), cols),
                   pl.BlockSpec((tm, GATE_LANES), rows),
                   pl.BlockSpec((GATE_ROWS, tm), cols)],
        out_shape=[jax.ShapeDtypeStruct((n, WIDE_WIDTH), BF16),
                   jax.ShapeDtypeStruct((ML_WIDTH, n), BF16),
                   jax.ShapeDtypeStruct((RET_WIDTH, n), BF16),
                   jax.ShapeDtypeStruct((n, GATE_LANES), F32),
                   jax.ShapeDtypeStruct((GATE_ROWS, n), F32)],
        scratch_shapes=[pltpu.VMEM((tm, D_MODEL), BF16),
                        pltpu.VMEM((TAIL + tm, CONV_WIDTH), F32)],
        compiler_params=pltpu.CompilerParams(
            dimension_semantics=("arbitrary",), vmem_limit_bytes=VMEM_LIMIT),
        name="inproj",
    )(x, g, w_wide, w_vml_t, w_vret_t, w_nar, w_nar_t, conv_w, conv_b)


def _mlstm_kernel(*refs, bsz):
    q_ref, k_ref, o_ref, gates_ref = refs[0:4]
    vt_refs = refs[4:4 + bsz]
    gatest_refs = refs[4 + bsz:4 + 2 * bsz]
    brow_ref, bcol_ref, norm_ref, y_ref, ct_ref, n_ref, m_ref = refs[4 + 2 * bsz:]

    @pl.when(pl.program_id(0) == 0)
    def _():
        ct_ref[...] = jnp.zeros(ct_ref.shape, F32)
        n_ref[...] = jnp.zeros(n_ref.shape, F32)
        m_ref[...] = jnp.zeros(m_ref.shape, F32)

    row, col = _chunk_iotas()
    causal_t = row <= col
    tri = (col <= row).astype(F32)
    tri_t = causal_t.astype(F32)

    gate_terms = []
    for b in range(bsz):
        g_col = gates_ref[b] + brow_ref[...]
        g_row = gatest_refs[b][0:ML_GATE_ROWS, :] + bcol_ref[0:ML_GATE_ROWS, :]
        b_col = _dot_exact(tri, _log_sigmoid(g_col))
        b_row = _dot_exact(_log_sigmoid(g_row), tri_t)
        gate_terms.append((g_row, b_row, g_col - pltpu.roll(b_col, GATE_LANES - F_LANE, 1)))

    units = [(b, h) for b in range(bsz) for h in range(ML_HEADS)]
    head = lambda h: slice(h * ML_HEAD_DIM, (h + 1) * ML_HEAD_DIM)

    stage1 = []
    for st, (b, h) in enumerate(units):
        g_row, b_row, cs_all = gate_terms[b]
        q = q_ref[b, :, head(h)]
        k = k_ref[b, :, head(h)]
        m_prev = m_ref[st][:, 0:1]
        b_t = b_row[F_LANE + h:F_LANE + h + 1, :]
        cs = cs_all[:, h:h + 1]
        log_d = jnp.where(causal_t, b_t + cs, -jnp.inf)
        a = b_t + m_prev
        m_t = jnp.maximum(a, jnp.max(log_d, axis=0, keepdims=True))
        inter = jnp.exp(a - m_t)
        scores = _dot_nt(k, q) * jnp.exp(log_d - m_t)
        den = jnp.sum(scores, axis=0, keepdims=True) + inter * _row_vec_dot(n_ref[st], mat_nt=q)
        scale = 1.0 / jnp.maximum(jnp.abs(den), jnp.exp(-m_t))
        cross = _dot_nt(ct_ref[st].astype(BF16), q) * (inter * scale)
        stage1.append((scores.astype(BF16), cross, scale))

    outs = []
    for st, (b, h) in enumerate(units):
        scores, cross, scale = stage1[st]
        out = _dot(vt_refs[b][head(h), :], scores) * scale + cross
        outs.append(out * lax.rsqrt(jnp.mean(out * out, axis=0, keepdims=True) + NORM_EPS))

    for st, (b, h) in enumerate(units):
        g_row, b_row, _ = gate_terms[b]
        k = k_ref[b, :, head(h)]
        vt = vt_refs[b][head(h), :]
        m_prev = m_ref[st][:, 0:1]
        b_t = b_row[F_LANE + h:F_LANE + h + 1, :]
        i_t = g_row[I_LANE + h:I_LANE + h + 1, :]
        b_last = b_t[:, CHUNK - 1:CHUNK]
        w_state = b_last - b_t + i_t
        m_new = jnp.maximum(b_last + m_prev, jnp.max(w_state, axis=1, keepdims=True))
        a_old = jnp.exp(b_last + m_prev - m_new)
        e = jnp.exp(w_state - m_new)
        ct_ref[st] = a_old * ct_ref[st] + _dot((vt.astype(F32) * e).astype(BF16), k)
        n_ref[st] = a_old * n_ref[st] + _row_vec_dot(e, mat=k)
        m_ref[st] = jnp.broadcast_to(m_new, m_ref.shape[1:])

    for st, (b, h) in enumerate(units):
        y_ref[b, :, head(h)] = (outs[st].T * norm_ref[:, head(h)]
                                * o_ref[b, :, head(h)].astype(F32)).astype(BF16)


def _per_batch_specs(rows, bsz, nc):
    return [pl.BlockSpec((rows, CHUNK), functools.partial(lambda b, c: (0, b * nc + c), b))
            for b in range(bsz)]


def _mlstm(layer, proj, v_t, gates, gates_t, bias_row, bias_col, norm, bsz, nc):
    seq = nc * CHUNK
    tok3 = lambda blk: (lambda c: (0, c, blk))
    per_batch = lambda rows: _per_batch_specs(rows, bsz, nc)
    return pl.pallas_call(
        functools.partial(_mlstm_kernel, bsz=bsz),
        grid=(nc,),
        in_specs=[pl.BlockSpec((bsz, CHUNK, ML_WIDTH), tok3(COL_ML_Q // ML_WIDTH)),
                  pl.BlockSpec((bsz, CHUNK, ML_WIDTH), tok3(COL_ML_K // ML_WIDTH)),
                  pl.BlockSpec((bsz, CHUNK, ML_WIDTH), tok3(COL_ML_O // ML_WIDTH)),
                  pl.BlockSpec((bsz, CHUNK, GATE_LANES), tok3(0))]
                 + per_batch(ML_WIDTH) + per_batch(GATE_ROWS)
                 + [_layer_spec(layer, (1, GATE_LANES)),
                    _layer_spec(layer, (GATE_ROWS, 1)),
                    _layer_spec(layer, (1, ML_WIDTH))],
        out_specs=pl.BlockSpec((bsz, CHUNK, ML_WIDTH), tok3(0)),
        out_shape=jax.ShapeDtypeStruct((bsz, seq, ML_WIDTH), BF16),
        scratch_shapes=[pltpu.VMEM((bsz * ML_HEADS, ML_HEAD_DIM, ML_HEAD_DIM), F32),
                        pltpu.VMEM((bsz * ML_HEADS, 1, ML_HEAD_DIM), F32),
                        pltpu.VMEM((bsz * ML_HEADS, 1, 128), F32)],
        compiler_params=pltpu.CompilerParams(
            dimension_semantics=("arbitrary",), vmem_limit_bytes=VMEM_LIMIT),
        name="mlstm",
    )(proj, proj, proj, gates, *([v_t] * bsz), *([gates_t] * bsz), bias_row, bias_col, norm)


def _ssd_kernel(*refs, bsz, layer):
    d_ref, x_ref, b_ref, c_ref, z_ref, gates_ref = refs[0:6]
    gatest_refs = refs[6:6 + bsz]
    (brow_ref, bcol_ref, alog_row_ref, alog_col_ref, norm_ref, y_ref,
     st_ref, xt_buf, yt_buf) = refs[6 + bsz:]

    @pl.when(pl.program_id(0) == 0)
    def _():
        st_ref[...] = jnp.zeros(st_ref.shape, F32)

    row, col = _chunk_iotas()
    causal_t = row <= col
    tri = (col <= row).astype(F32)
    tri_t = causal_t.astype(F32)
    rows16 = slice(DT_ROW0, DT_ROW0 + 16)
    a_neg_row = -jnp.exp(alog_row_ref[...])
    a_neg_col = -jnp.exp(alog_col_ref[rows16, :])

    gate_terms = []
    group_terms = []
    for b in range(bsz):
        dt_col = _softplus(gates_ref[b] + brow_ref[...])
        dt_row = _softplus(gatest_refs[b][rows16, :] + bcol_ref[rows16, :])
        acum_col = _dot_exact(tri, dt_col * a_neg_row)
        acum_row = _dot_exact(dt_row * a_neg_col, tri_t)
        gate_terms.append((dt_row, acum_col, acum_row))
        for blk in range(0, SSD_WIDTH, 128):
            xt_buf[b, blk:blk + 128, :] = x_ref[b, :, blk:blk + 128].astype(F32).T
        for g in range(SSD_GROUPS):
            gs = slice(g * SSD_STATE, (g + 1) * SSD_STATE)
            b_g = b_ref[b, :, gs]
            c_g = c_ref[b, :, gs]
            prev = st_ref[b * SSD_GROUPS + g]
            group_terms.append((b_g, prev, _dot_nt(b_g, c_g),
                                _dot_nt(prev.astype(BF16), c_g)))

    units = [(b, h) for b in range(bsz) for h in range(SSD_HEADS)]
    head_terms = []
    for b, h in units:
        dt_row, acum_col, acum_row = gate_terms[b]
        cb = group_terms[b * SSD_GROUPS + h // SSD_HEADS_PER_GROUP][2]
        r = DT_LANE + h - DT_ROW0
        lane = DT_LANE + h
        a_t = acum_row[r:r + 1, :]
        a_s = acum_col[:, lane:lane + 1]
        l_dec = jnp.exp(jnp.where(causal_t, a_t - a_s, -jnp.inf))
        x_h = xt_buf[b, h * SSD_HEAD_DIM:(h + 1) * SSD_HEAD_DIM, :]
        xc = x_h * dt_row[r:r + 1, :]
        a_last = a_t[:, CHUNK - 1:CHUNK]
        head_terms.append(((cb * l_dec).astype(BF16), x_h, xc.astype(BF16),
                           (xc * jnp.exp(a_last - a_t)).astype(BF16), jnp.exp(a_t), jnp.exp(a_last)))

    for i, (b, h) in enumerate(units):
        m_h, x_h, xc_b, _, exp_a, _ = head_terms[i]
        g, hh = divmod(h, SSD_HEADS_PER_GROUP)
        y_off = group_terms[b * SSD_GROUPS + g][3]
        hs = slice(h * SSD_HEAD_DIM, (h + 1) * SSD_HEAD_DIM)
        yt_buf[b, hs, :] = (_dot(xc_b, m_h)
                            + y_off[hh * SSD_HEAD_DIM:(hh + 1) * SSD_HEAD_DIM, :] * exp_a
                            + d_ref[layer, h] * x_h)

    for b in range(bsz):
        for g in range(SSD_GROUPS):
            st = b * SSD_GROUPS + g
            b_g, prev = group_terms[st][0:2]
            terms = head_terms[b * SSD_HEADS + g * SSD_HEADS_PER_GROUP:
                               b * SSD_HEADS + (g + 1) * SSD_HEADS_PER_GROUP]
            upd = _dot(jnp.concatenate([t[3] for t in terms], axis=0), b_g)
            for hh in range(SSD_HEADS_PER_GROUP):
                ps = slice(hh * SSD_HEAD_DIM, (hh + 1) * SSD_HEAD_DIM)
                st_ref[st, ps, :] = terms[hh][5] * prev[ps, :] + upd[ps, :]

    for b in range(bsz):
        for g in range(SSD_GROUPS):
            parts = []
            for blk in range(g * SSD_GROUP_WIDTH, (g + 1) * SSD_GROUP_WIDTH, 128):
                parts.append(yt_buf[b, blk:blk + 128, :].T * z_ref[b, :, blk:blk + 128].astype(F32))
            ssq = sum(jnp.sum(p * p, axis=1, keepdims=True) for p in parts)
            inv = lax.rsqrt(ssq * (1.0 / SSD_GROUP_WIDTH) + NORM_EPS)
            for i, p in enumerate(parts):
                cs = slice(g * SSD_GROUP_WIDTH + i * 128, g * SSD_GROUP_WIDTH + (i + 1) * 128)
                y_ref[b, :, cs] = (p * inv * norm_ref[:, cs]).astype(BF16)


def _ssd(layer, d, proj, gates, gates_t, bias_row, bias_col, alog_row, alog_col, norm, bsz, nc):
    tok3 = lambda blk: (lambda c: (0, c, blk))
    return pl.pallas_call(
        functools.partial(_ssd_kernel, bsz=bsz, layer=layer),
        grid=(nc,),
        in_specs=[pl.BlockSpec(memory_space=pltpu.SMEM),
                  pl.BlockSpec((bsz, CHUNK, SSD_WIDTH), tok3(COL_SSD_X // SSD_WIDTH)),
                  pl.BlockSpec((bsz, CHUNK, SSD_BC_WIDTH), tok3(COL_SSD_B // SSD_BC_WIDTH)),
                  pl.BlockSpec((bsz, CHUNK, SSD_BC_WIDTH), tok3(COL_SSD_C // SSD_BC_WIDTH)),
                  pl.BlockSpec((bsz, CHUNK, SSD_WIDTH), tok3(COL_SSD_Z // SSD_WIDTH)),
                  pl.BlockSpec((bsz, CHUNK, GATE_LANES), tok3(0))]
                 + _per_batch_specs(GATE_ROWS, bsz, nc)
                 + [_layer_spec(layer, (1, GATE_LANES)),
                    _layer_spec(layer, (GATE_ROWS, 1)),
                    _layer_spec(layer, (1, GATE_LANES)),
                    _layer_spec(layer, (GATE_ROWS, 1)),
                    _layer_spec(layer, (1, SSD_WIDTH))],
        out_specs=pl.BlockSpec((bsz, CHUNK, SSD_WIDTH), tok3(0)),
        out_shape=jax.ShapeDtypeStruct((bsz, nc * CHUNK, SSD_WIDTH), BF16),
        scratch_shapes=[pltpu.VMEM((bsz * SSD_GROUPS, SSD_GROUP_WIDTH, SSD_STATE), F32),
                        pltpu.VMEM((bsz, SSD_WIDTH, CHUNK), F32),
                        pltpu.VMEM((bsz, SSD_WIDTH, CHUNK), F32)],
        compiler_params=pltpu.CompilerParams(
            dimension_semantics=("arbitrary",), vmem_limit_bytes=VMEM_LIMIT),
        name="ssd",
    )(d, proj, proj, proj, proj, gates, *([gates_t] * bsz), bias_row, bias_col, alog_row, alog_col,
      norm)


def _ret_kernel(*refs, bsz):
    q_ref, k_ref, g_ref, cc_ref, ss_ref = refs[0:5]
    vt_refs = refs[5:5 + bsz]
    norm_ref, y_ref, rt_ref, dmat_ref = refs[5 + bsz:]
    log_gammas = [math.log1p(-2.0 ** (-5.0 - h)) for h in range(RET_HEADS)]

    @pl.when(pl.program_id(0) == 0)
    def _():
        rt_ref[...] = jnp.zeros(rt_ref.shape, F32)
        row, col = _chunk_iotas()
        rel = (col - row).astype(F32)
        for h in range(RET_HEADS):
            dmat_ref[h] = jnp.exp(jnp.where(row <= col, log_gammas[h] * rel, -jnp.inf))

    t_row = lax.broadcasted_iota(jnp.int32, (1, CHUNK), 1).astype(F32)
    half = RET_HEAD_DIM // 2
    scale = RET_HEAD_DIM ** -0.5

    roped = []
    for b in range(bsz):
        cc = cc_ref[b]
        ss = ss_ref[b]
        for h in range(RET_HEADS):
            sl = slice(h * RET_HEAD_DIM, (h + 1) * RET_HEAD_DIM)
            q = q_ref[b, :, sl].astype(F32)
            k = k_ref[b, :, sl].astype(F32)
            roped.append(((q * cc + pltpu.roll(q, half, 1) * ss).astype(BF16),
                          ((k * cc + pltpu.roll(k, half, 1) * ss) * scale).astype(BF16)))

    units = [(b, h) for b in range(bsz) for h in range(RET_HEADS)]
    xis = [jnp.exp(lg * (t_row + 1.0)) for lg in log_gammas]
    zetas = [jnp.exp(lg * (CHUNK - 1.0 - t_row)) for lg in log_gammas]
    scores, cross = [], []
    for st, (b, h) in enumerate(units):
        q, k = roped[st]
        scores.append((_dot_nt(k, q) * dmat_ref[h]).astype(BF16))
        cross.append(_dot_nt(rt_ref[st].astype(BF16), q) * xis[h])
    ys = []
    for st, (b, h) in enumerate(units):
        sl = slice(h * RET_HEAD_DIM, (h + 1) * RET_HEAD_DIM)
        _, k = roped[st]
        vt = vt_refs[b][sl, :]
        ys.append(_dot(vt, scores[st]) + cross[st])
        rt_ref[st] = (math.exp(log_gammas[h] * CHUNK) * rt_ref[st]
                      + _dot((vt.astype(F32) * zetas[h]).astype(BF16), k))
    for st, (b, h) in enumerate(units):
        sl = slice(h * RET_HEAD_DIM, (h + 1) * RET_HEAD_DIM)
        y = ys[st]
        yc = y - jnp.mean(y, axis=0, keepdims=True)
        yn = yc * lax.rsqrt(jnp.mean(yc * yc, axis=0, keepdims=True) + NORM_EPS)
        y_ref[b, :, sl] = (yn.T * norm_ref[:, sl] * g_ref[b, :, sl].astype(F32)).astype(BF16)


def _retention(layer, proj, v_t, cc, ss, norm, bsz, nc):
    tok3 = lambda blk: (lambda c: (0, c, blk))
    return pl.pallas_call(
        functools.partial(_ret_kernel, bsz=bsz),
        grid=(nc,),
        in_specs=[pl.BlockSpec((bsz, CHUNK, RET_WIDTH), tok3(COL_RET_Q // RET_WIDTH)),
                  pl.BlockSpec((bsz, CHUNK, RET_WIDTH), tok3(COL_RET_K // RET_WIDTH)),
                  pl.BlockSpec((bsz, CHUNK, RET_WIDTH), tok3(COL_RET_G // RET_WIDTH)),
                  pl.BlockSpec((bsz, CHUNK, RET_HEAD_DIM), tok3(0)),
                  pl.BlockSpec((bsz, CHUNK, RET_HEAD_DIM), tok3(0))]
                 + _per_batch_specs(RET_WIDTH, bsz, nc)
                 + [_layer_spec(layer, (1, RET_WIDTH))],
        out_specs=pl.BlockSpec((bsz, CHUNK, RET_WIDTH), tok3(0)),
        out_shape=jax.ShapeDtypeStruct((bsz, nc * CHUNK, RET_WIDTH), BF16),
        scratch_shapes=[pltpu.VMEM((bsz * RET_HEADS, RET_HEAD_DIM, RET_HEAD_DIM), F32),
                        pltpu.VMEM((RET_HEADS, CHUNK, CHUNK), F32)],
        compiler_params=pltpu.CompilerParams(
            dimension_semantics=("arbitrary",), vmem_limit_bytes=VMEM_LIMIT),
        name="retention",
    )(proj, proj, proj, cc, ss, *([v_t] * bsz), norm)


FFN_BLOCK = 256


def _out_ffn_kernel(x_ref, yml_ref, yssd_ref, yret_ref, wo_ref, g_ref, wgu_ref, wd_ref,
                    gfin_ref, o_ref, *, final_norm):
    mix = jnp.concatenate([yml_ref[...], yssd_ref[...], yret_ref[...]], axis=1)
    x1 = x_ref[...] + _dot(mix, wo_ref[...])
    h = (x1 * lax.rsqrt(jnp.mean(x1 * x1, axis=-1, keepdims=True) + NORM_EPS) * g_ref[...]).astype(BF16)
    acc = x1
    for blk in range(0, FFN_HIDDEN, FFN_BLOCK):
        gate = _dot(h, wgu_ref[:, blk:blk + FFN_BLOCK])
        up = _dot(h, wgu_ref[:, FFN_HIDDEN + blk:FFN_HIDDEN + blk + FFN_BLOCK])
        acc = acc + _dot((_silu(gate) * up).astype(BF16), wd_ref[blk:blk + FFN_BLOCK, :])
    if final_norm:
        acc = acc * lax.rsqrt(jnp.mean(acc * acc, axis=-1, keepdims=True) + NORM_EPS) * gfin_ref[...]
    o_ref[...] = acc


def _out_ffn(layer, x, y_ml, y_ssd, y_ret, w_out, g, w_gu, w_down, g_final, final_norm):
    n = x.shape[0]
    tm = min(n, 512)
    assert n % tm == 0
    rows = lambda i: (i, 0)
    const = lambda i: (0, 0)
    resident = functools.partial(_layer_spec, layer, pipeline_mode=pl.Buffered(1))
    return pl.pallas_call(
        functools.partial(_out_ffn_kernel, final_norm=final_norm),
        grid=(n // tm,),
        in_specs=[pl.BlockSpec((tm, D_MODEL), rows),
                  pl.BlockSpec((tm, ML_WIDTH), rows),
                  pl.BlockSpec((tm, SSD_WIDTH), rows),
                  pl.BlockSpec((tm, RET_WIDTH), rows),
                  resident((MIX_WIDTH, D_MODEL)),
                  _layer_spec(layer, (1, D_MODEL)),
                  resident((D_MODEL, 2 * FFN_HIDDEN)),
                  resident((FFN_HIDDEN, D_MODEL)),
                  pl.BlockSpec((1, D_MODEL), const)],
        out_specs=pl.BlockSpec((tm, D_MODEL), rows),
        out_shape=jax.ShapeDtypeStruct((n, D_MODEL), F32),
        compiler_params=pltpu.CompilerParams(
            dimension_semantics=("parallel",), vmem_limit_bytes=VMEM_LIMIT),
        name="out_ffn",
    )(x, y_ml, y_ssd, y_ret, w_out, g, w_gu, w_down, g_final)


def _gate_row(v, offset):
    depth, k = v.shape
    return jnp.zeros((depth, 1, GATE_LANES), F32).at[:, 0, offset:offset + k].set(v)


def _gate_col(v, offset):
    depth, k = v.shape
    return jnp.zeros((depth, GATE_ROWS, 1), F32).at[:, offset:offset + k, 0].set(v)


def kernel(x, positions, norm_mix, w_in, ml_conv_w, ml_conv_b, ml_gate_bias, ml_norm,
           ssd_conv_w, ssd_conv_b, ssd_dt_bias, ssd_a_log, ssd_d, ssd_norm, ret_norm,
           w_out, norm_ffn, w_gate_up, w_down, norm_final):
    bsz, seq, _ = x.shape
    depth = w_in.shape[0]
    n = bsz * seq
    nc = seq // CHUNK
    assert seq % CHUNK == 0 and x.shape[2] == D_MODEL and w_in.shape[2] == ML_IN + SSD_IN + RET_IN

    ml0, ssd0, ret0 = 0, ML_IN, ML_IN + SSD_IN
    w_in_b = w_in.astype(BF16)
    seg = lambda start, width: w_in_b[:, :, start:start + width]
    ml_q, ml_k = seg(ml0, ML_WIDTH), seg(ml0 + ML_WIDTH, ML_WIDTH)
    ml_v, ml_o = seg(ml0 + 2 * ML_WIDTH, ML_WIDTH), seg(ml0 + 3 * ML_WIDTH, ML_WIDTH)
    ml_if = seg(ml0 + 4 * ML_WIDTH, 2 * ML_HEADS)
    ssd_z, ssd_x = seg(ssd0, SSD_WIDTH), seg(ssd0 + SSD_WIDTH, SSD_WIDTH)
    ssd_bc = seg(ssd0 + 2 * SSD_WIDTH, 2 * SSD_BC_WIDTH)
    ssd_dt = seg(ssd0 + SSD_WIDTH + SSD_CONV_DIM, SSD_HEADS)
    ret_qk, ret_v = seg(ret0, 2 * RET_WIDTH), seg(ret0 + 2 * RET_WIDTH, RET_WIDTH)
    ret_g = seg(ret0 + 3 * RET_WIDTH, RET_WIDTH)
    w_wide = jnp.concatenate([ret_qk, ret_g, ml_q, ml_k, ssd_x, ml_o, ssd_z, ssd_bc], axis=-1)
    w_vml_t = jnp.swapaxes(ml_v, 1, 2)
    w_vret_t = jnp.swapaxes(ret_v, 1, 2)
    w_gate_cols = jnp.concatenate([ml_if, ssd_dt], axis=-1)
    n_gate = w_gate_cols.shape[-1]
    w_nar = jnp.pad(w_gate_cols, ((0, 0), (0, 0), (0, GATE_LANES - n_gate)))
    w_nar_t = jnp.pad(jnp.swapaxes(w_gate_cols, 1, 2), ((0, 0), (0, GATE_ROWS - n_gate), (0, 0)))
    conv_w = jnp.concatenate([ml_conv_w, ssd_conv_w], axis=-1)
    conv_b = jnp.concatenate([ml_conv_b, ssd_conv_b], axis=-1)[:, None, :]
    w_out_b = w_out.astype(BF16)
    w_gu_b = w_gate_up.astype(BF16)
    w_down_b = w_down.astype(BF16)

    gate_bias = jnp.concatenate([ml_gate_bias, ssd_dt_bias], axis=-1)
    bias_row, bias_col = _gate_row(gate_bias, 0), _gate_col(gate_bias, 0)
    alog_row, alog_col = _gate_row(ssd_a_log, DT_LANE), _gate_col(ssd_a_log, DT_LANE)
    row3 = lambda v: v[:, None, :]
    norm_mix3, norm_ffn3 = row3(norm_mix), row3(norm_ffn)
    ml_norm3, ssd_norm3, ret_norm3 = row3(ml_norm), row3(ssd_norm), row3(ret_norm)
    g_final = norm_final.reshape(1, D_MODEL)

    cc, ss = _rope_tables(positions)
    cc = cc.reshape(bsz, seq, RET_HEAD_DIM)
    ss = ss.reshape(bsz, seq, RET_HEAD_DIM)
    xf = x.reshape(n, D_MODEL)
    for l in range(depth):
        proj, vml_t, vret_t, gates, gates_t = _inproj(
            l, xf, norm_mix3, w_wide, w_vml_t, w_vret_t, w_nar, w_nar_t, conv_w, conv_b, seq)
        proj = proj.reshape(bsz, seq, WIDE_WIDTH)
        gates = gates.reshape(bsz, seq, GATE_LANES)
        y_ml = _mlstm(l, proj, vml_t, gates, gates_t, bias_row, bias_col, ml_norm3, bsz, nc)
        y_ssd = _ssd(l, ssd_d, proj, gates, gates_t, bias_row, bias_col, alog_row, alog_col,
                     ssd_norm3, bsz, nc)
        y_ret = _retention(l, proj, vret_t, cc, ss, ret_norm3, bsz, nc)
        xf = _out_ffn(l, xf, y_ml.reshape(n, ML_WIDTH), y_ssd.reshape(n, SSD_WIDTH),
                      y_ret.reshape(n, RET_WIDTH), w_out_b, norm_ffn3, w_gu_b, w_down_b,
                      g_final, final_norm=(l == depth - 1))
    return xf.reshape(bsz, seq, D_MODEL)
```

```python
import functools
import math

import jax
import jax.numpy as jnp
from jax import lax
from jax.experimental import pallas as pl
from jax.experimental.pallas import tpu as pltpu

F32 = jnp.float32
BF16 = jnp.bfloat16

D_MODEL = 1024
CHUNK = 128
CONV_K = 4
NORM_EPS = 1e-6
ML_HEADS = 6
ML_HEAD_DIM = 128
ML_WIDTH = ML_HEADS * ML_HEAD_DIM
SSD_HEADS = 12
SSD_HEAD_DIM = 64
SSD_WIDTH = SSD_HEADS * SSD_HEAD_DIM
SSD_GROUPS = 2
SSD_HEADS_PER_GROUP = SSD_HEADS // SSD_GROUPS
SSD_GROUP_WIDTH = SSD_HEADS_PER_GROUP * SSD_HEAD_DIM
SSD_STATE = 128
SSD_BC_WIDTH = SSD_GROUPS * SSD_STATE
SSD_CONV_DIM = SSD_WIDTH + 2 * SSD_BC_WIDTH
RET_HEADS = 4
RET_HEAD_DIM = 128
RET_WIDTH = RET_HEADS * RET_HEAD_DIM
ROPE_BASE = 10000.0
MIX_WIDTH = ML_WIDTH + SSD_WIDTH + RET_WIDTH
Y_ML, Y_SSD, Y_RET = 0, ML_WIDTH, ML_WIDTH + SSD_WIDTH
ML_IN = 4 * ML_WIDTH + 2 * ML_HEADS
SSD_IN = SSD_WIDTH + SSD_CONV_DIM + SSD_HEADS
RET_IN = 4 * RET_WIDTH
FFN_HIDDEN = 2816

COL_RET_Q = 0
COL_RET_K = COL_RET_Q + RET_WIDTH
COL_RET_G = COL_RET_K + RET_WIDTH
COL_ML_Q = COL_RET_G + RET_WIDTH
COL_ML_K = COL_ML_Q + ML_WIDTH
COL_SSD_X = COL_ML_K + ML_WIDTH
COL_ML_O = COL_SSD_X + SSD_WIDTH
COL_SSD_Z = COL_ML_O + ML_WIDTH
COL_SSD_B = COL_SSD_Z + SSD_WIDTH
COL_SSD_C = COL_SSD_B + SSD_BC_WIDTH
WIDE_WIDTH = COL_SSD_C + SSD_BC_WIDTH
PROJ_BLOCK = 256
SEGMENTS = (
    (COL_RET_Q, 2 * RET_WIDTH, "plain"),
    (COL_RET_G, RET_WIDTH, "silu"),
    (COL_ML_Q, ML_WIDTH, "conv"),
    (COL_ML_K, ML_WIDTH, "conv_scaled"),
    (COL_SSD_X, SSD_WIDTH, "conv"),
    (COL_ML_O, ML_WIDTH, "sigmoid"),
    (COL_SSD_Z, SSD_WIDTH, "silu"),
    (COL_SSD_B, 2 * SSD_BC_WIDTH, "conv"),
)
CONV_WIDTH = 2 * ML_WIDTH + SSD_CONV_DIM

GATE_LANES = 128
GATE_ROWS = 32
I_LANE = 0
F_LANE = ML_HEADS
DT_LANE = 2 * ML_HEADS
ML_GATE_ROWS = 16
DT_ROW0 = 8
TAIL = 8

VMEM_LIMIT = 56 * 1024 * 1024

_NT = (((1,), (1,)), ((), ()))


def _sigmoid(x):
    return 1.0 / (1.0 + jnp.exp2(x * (-1.0 / math.log(2.0))))


def _silu(x):
    return x * _sigmoid(x)


def _log_sigmoid(x):
    return jnp.minimum(x, 0.0) - jnp.log1p(jnp.exp(-jnp.abs(x)))


def _softplus(x):
    return jnp.maximum(x, 0.0) + jnp.log1p(jnp.exp(-jnp.abs(x)))


def _dot(a, b):
    return jnp.dot(a, b, preferred_element_type=F32)


def _dot_nt(a, b):
    return lax.dot_general(a, b, _NT, preferred_element_type=F32)


def _dot_exact(a, b):
    return jnp.dot(a, b, preferred_element_type=F32, precision=lax.Precision.HIGHEST)


def _row_vec_dot(vec, mat_nt=None, mat=None):
    rows = 16
    v = jnp.broadcast_to(vec, (rows, vec.shape[1]))
    hi = v.astype(BF16).astype(F32)
    r = lax.broadcasted_iota(jnp.int32, v.shape, 0)
    lhs = jnp.where(r == 0, hi, jnp.where(r == 1, v - hi, 0.0)).astype(BF16)
    out = _dot_nt(lhs, mat_nt) if mat_nt is not None else _dot(lhs, mat)
    return jnp.sum(out, axis=0, keepdims=True)


def _chunk_iotas():
    row = lax.broadcasted_iota(jnp.int32, (CHUNK, CHUNK), 0)
    col = lax.broadcasted_iota(jnp.int32, (CHUNK, CHUNK), 1)
    return row, col


def _rope_kernel(pos_ref, invf_ref, cc_ref, ss_ref):
    ang = pos_ref[...].astype(F32) * invf_ref[...]
    lane = lax.broadcasted_iota(jnp.int32, ang.shape, 1)
    sin = jnp.sin(ang)
    cc_ref[...] = jnp.cos(ang)
    ss_ref[...] = jnp.where(lane < RET_HEAD_DIM // 2, -sin, sin)


def _rope_tables(positions):
    n = positions.size
    tm = min(n, 2048)
    assert n % tm == 0
    half = RET_HEAD_DIM // 2
    inv_freq = ROPE_BASE ** (-jnp.arange(0, RET_HEAD_DIM, 2, dtype=F32) / RET_HEAD_DIM)
    invf = jnp.concatenate([inv_freq, inv_freq]).reshape(1, 2 * half)
    return pl.pallas_call(
        _rope_kernel,
        grid=(n // tm,),
        in_specs=[pl.BlockSpec((tm, 1), lambda i: (i, 0)),
                  pl.BlockSpec((1, RET_HEAD_DIM), lambda i: (0, 0))],
        out_specs=[pl.BlockSpec((tm, RET_HEAD_DIM), lambda i: (i, 0))] * 2,
        out_shape=[jax.ShapeDtypeStruct((n, RET_HEAD_DIM), F32)] * 2,
        name="rope_tables",
    )(positions.reshape(n, 1), invf)


def _inproj_kernel(x_ref, g_ref, w_ref, wvml_ref, wvret_ref, wn_ref, wnt_ref, cw_ref, cb_ref,
                   o_ref, vml_ref, vret_ref, gates_ref, gatest_ref, h_ref, cbuf, *, tiles_per_seq):
    tm = x_ref.shape[0]

    @pl.when(pl.program_id(0) % tiles_per_seq == 0)
    def _():
        cbuf[0:TAIL, :] = jnp.zeros((TAIL, cbuf.shape[1]), F32)

    x = x_ref[...]
    y = x * lax.rsqrt(jnp.mean(x * x, axis=-1, keepdims=True) + NORM_EPS) * g_ref[...]
    h_ref[...] = y.astype(BF16)
    gates_ref[...] = _dot(h_ref[...], wn_ref[...])
    gatest_ref[...] = _dot_nt(wnt_ref[...], h_ref[...])

    def conv_epilogue(acc, cols, cc, scaled):
        cbuf[TAIL:TAIL + tm, cc] = acc
        out = cb_ref[:, cc] + cw_ref[CONV_K - 1:CONV_K, cc] * acc
        for j in range(1, CONV_K):
            out = out + cw_ref[CONV_K - 1 - j:CONV_K - j, cc] * cbuf[TAIL - j:TAIL - j + tm, cc]
        cbuf[0:TAIL, cc] = cbuf[tm:tm + TAIL, cc]
        out = _silu(out)
        if scaled:
            out = out * (ML_HEAD_DIM ** -0.5)
        o_ref[:, cols] = out.astype(BF16)

    def plain_epilogue(acc, cols, act):
        o_ref[:, cols] = (acc if act is None else act(acc)).astype(BF16)

    def t_epilogue(acc, ref, rows):
        ref[rows, :] = acc.astype(BF16)

    heavy, light = [], []
    conv_col = 0
    for start, width, kind in SEGMENTS:
        for blk in range(start, start + width, PROJ_BLOCK):
            cols = slice(blk, blk + PROJ_BLOCK)
            mm = functools.partial(lambda c: _dot(h_ref[...], w_ref[:, c]), cols)
            if kind in ("conv", "conv_scaled"):
                cc = slice(conv_col, conv_col + PROJ_BLOCK)
                conv_col += PROJ_BLOCK
                heavy.append((mm, functools.partial(conv_epilogue, cols=cols, cc=cc,
                                                    scaled=(kind == "conv_scaled"))))
            else:
                act = {"silu": _silu, "sigmoid": _sigmoid, "plain": None}[kind]
                light.append((mm, functools.partial(plain_epilogue, cols=cols, act=act)))
    for w_t_ref, out_ref, width in ((wvml_ref, vml_ref, ML_WIDTH), (wvret_ref, vret_ref, RET_WIDTH)):
        for blk in range(0, width, PROJ_BLOCK):
            rows = slice(blk, blk + PROJ_BLOCK)
            mm = functools.partial(lambda r, w: _dot_nt(w[r, :], h_ref[...]), rows, w_t_ref)
            light.append((mm, functools.partial(t_epilogue, ref=out_ref, rows=rows)))

    jobs = []
    while heavy or light:
        if light:
            jobs.append(light.pop(0))
        if heavy:
            jobs.append(heavy.pop(0))
    acc = jobs[0][0]()
    for i, (_, epilogue) in enumerate(jobs):
        nxt = jobs[i + 1][0]() if i + 1 < len(jobs) else None
        epilogue(acc)
        acc = nxt


def _layer_spec(layer, shape, **kwargs):
    return pl.BlockSpec((None,) + tuple(shape), lambda *_: (layer,) + (0,) * len(shape), **kwargs)


def _inproj(layer, x, g, w_wide, w_vml_t, w_vret_t, w_nar, w_nar_t, conv_w, conv_b, seq):
    n = x.shape[0]
    tm = min(seq, 512)
    assert seq % tm == 0 and n % seq == 0
    rows = lambda i: (i, 0)
    cols = lambda i: (0, i)
    resident = functools.partial(_layer_spec, layer, pipeline_mode=pl.Buffered(1))
    return pl.pallas_call(
        functools.partial(_inproj_kernel, tiles_per_seq=seq // tm),
        grid=(n // tm,),
        in_specs=[pl.BlockSpec((tm, D_MODEL), rows),
                  _layer_spec(layer, (1, D_MODEL)),
                  resident((D_MODEL, WIDE_WIDTH)),
                  resident((ML_WIDTH, D_MODEL)),
                  resident((RET_WIDTH, D_MODEL)),
                  resident((D_MODEL, GATE_LANES)),
                  resident((GATE_ROWS, D_MODEL)),
                  _layer_spec(layer, (CONV_K, CONV_WIDTH)),
                  _layer_spec(layer, (1, CONV_WIDTH))],
        out_specs=[pl.BlockSpec((tm, WIDE_WIDTH), rows),
                   pl.BlockSpec((ML_WIDTH, tm), cols),
                   pl.BlockSpec((RET_WIDTH, tm), cols),
                   pl.BlockSpec((tm, GATE_LANES), rows),
                   pl.BlockSpec((GATE_ROWS, tm), cols)],
        out_shape=[jax.ShapeDtypeStruct((n, WIDE_WIDTH), BF16),
                   jax.ShapeDtypeStruct((ML_WIDTH, n), BF16),
                   jax.ShapeDtypeStruct((RET_WIDTH, n), BF16),
                   jax.ShapeDtypeStruct((n, GATE_LANES), F32),
                   jax.ShapeDtypeStruct((GATE_ROWS, n), F32)],
        scratch_shapes=[pltpu.VMEM((tm, D_MODEL), BF16),
                        pltpu.VMEM((TAIL + tm, CONV_WIDTH), F32)],
        compiler_params=pltpu.CompilerParams(
            dimension_semantics=("arbitrary",), vmem_limit_bytes=VMEM_LIMIT),
        name="inproj",
    )(x, g, w_wide, w_vml_t, w_vret_t, w_nar, w_nar_t, conv_w, conv_b)


def _mlstm_body(q_ref, k_ref, o_ref, gates_ref, vt_refs, gatest_refs, brow_ref, bcol_ref, norm_ref,
                y_ref, ct_ref, n_ref, m_ref):
    bsz = len(vt_refs)

    @pl.when(pl.program_id(0) == 0)
    def _():
        ct_ref[...] = jnp.zeros(ct_ref.shape, F32)
        n_ref[...] = jnp.zeros(n_ref.shape, F32)
        m_ref[...] = jnp.zeros(m_ref.shape, F32)

    row, col = _chunk_iotas()
    causal_t = row <= col
    tri = (col <= row).astype(F32)
    tri_t = causal_t.astype(F32)

    gate_terms = []
    for b in range(bsz):
        g_col = gates_ref[b] + brow_ref[...]
        g_row = gatest_refs[b][0:ML_GATE_ROWS, :] + bcol_ref[0:ML_GATE_ROWS, :]
        b_col = _dot_exact(tri, _log_sigmoid(g_col))
        b_row = _dot_exact(_log_sigmoid(g_row), tri_t)
        gate_terms.append((g_row, b_row, g_col - pltpu.roll(b_col, GATE_LANES - F_LANE, 1)))

    units = [(b, h) for b in range(bsz) for h in range(ML_HEADS)]
    head = lambda h: slice(h * ML_HEAD_DIM, (h + 1) * ML_HEAD_DIM)

    stage1 = []
    for st, (b, h) in enumerate(units):
        g_row, b_row, cs_all = gate_terms[b]
        q = q_ref[b, :, head(h)]
        k = k_ref[b, :, head(h)]
        m_prev = m_ref[st][:, 0:1]
        b_t = b_row[F_LANE + h:F_LANE + h + 1, :]
        cs = cs_all[:, h:h + 1]
        log_d = jnp.where(causal_t, b_t + cs, -jnp.inf)
        a = b_t + m_prev
        m_t = jnp.maximum(a, jnp.max(log_d, axis=0, keepdims=True))
        inter = jnp.exp(a - m_t)
        scores = _dot_nt(k, q) * jnp.exp(log_d - m_t)
        den = jnp.sum(scores, axis=0, keepdims=True) + inter * _row_vec_dot(n_ref[st], mat_nt=q)
        scale = 1.0 / jnp.maximum(jnp.abs(den), jnp.exp(-m_t))
        cross = _dot_nt(ct_ref[st].astype(BF16), q) * (inter * scale)
        stage1.append((scores.astype(BF16), cross, scale))

    outs = []
    for st, (b, h) in enumerate(units):
        scores, cross, scale = stage1[st]
        out = _dot(vt_refs[b][head(h), :], scores) * scale + cross
        outs.append(out * lax.rsqrt(jnp.mean(out * out, axis=0, keepdims=True) + NORM_EPS))

    for st, (b, h) in enumerate(units):
        g_row, b_row, _ = gate_terms[b]
        k = k_ref[b, :, head(h)]
        vt = vt_refs[b][head(h), :]
        m_prev = m_ref[st][:, 0:1]
        b_t = b_row[F_LANE + h:F_LANE + h + 1, :]
        i_t = g_row[I_LANE + h:I_LANE + h + 1, :]
        b_last = b_t[:, CHUNK - 1:CHUNK]
        w_state = b_last - b_t + i_t
        m_new = jnp.maximum(b_last + m_prev, jnp.max(w_state, axis=1, keepdims=True))
        a_old = jnp.exp(b_last + m_prev - m_new)
        e = jnp.exp(w_state - m_new)
        ct_ref[st] = a_old * ct_ref[st] + _dot((vt.astype(F32) * e).astype(BF16), k)
        n_ref[st] = a_old * n_ref[st] + _row_vec_dot(e, mat=k)
        m_ref[st] = jnp.broadcast_to(m_new, m_ref.shape[1:])

    for st, (b, h) in enumerate(units):
        y_ref[b, :, Y_ML + h * ML_HEAD_DIM:Y_ML + (h + 1) * ML_HEAD_DIM] = (
            outs[st].T * norm_ref[:, head(h)] * o_ref[b, :, head(h)].astype(F32)).astype(BF16)


def _per_batch_specs(rows, bsz, nc):
    return [pl.BlockSpec((rows, CHUNK), functools.partial(lambda b, c: (0, b * nc + c), b))
            for b in range(bsz)]


def _ssd_body(d_ref, x_ref, b_ref, c_ref, z_ref, gates_ref, gatest_refs, brow_ref, bcol_ref,
              alog_row_ref, alog_col_ref, norm_ref, y_ref, st_ref, xt_buf, yt_buf, layer):
    bsz = len(gatest_refs)

    @pl.when(pl.program_id(0) == 0)
    def _():
        st_ref[...] = jnp.zeros(st_ref.shape, F32)

    row, col = _chunk_iotas()
    causal_t = row <= col
    tri = (col <= row).astype(F32)
    tri_t = causal_t.astype(F32)
    rows16 = slice(DT_ROW0, DT_ROW0 + 16)
    a_neg_row = -jnp.exp(alog_row_ref[...])
    a_neg_col = -jnp.exp(alog_col_ref[rows16, :])

    gate_terms = []
    group_terms = []
    for b in range(bsz):
        dt_col = _softplus(gates_ref[b] + brow_ref[...])
        dt_row = _softplus(gatest_refs[b][rows16, :] + bcol_ref[rows16, :])
        acum_col = _dot_exact(tri, dt_col * a_neg_row)
        acum_row = _dot_exact(dt_row * a_neg_col, tri_t)
        gate_terms.append((dt_row, acum_col, acum_row))
        for blk in range(0, SSD_WIDTH, 128):
            xt_buf[b, blk:blk + 128, :] = x_ref[b, :, blk:blk + 128].astype(F32).T
        for g in range(SSD_GROUPS):
            gs = slice(g * SSD_STATE, (g + 1) * SSD_STATE)
            b_g = b_ref[b, :, gs]
            c_g = c_ref[b, :, gs]
            prev = st_ref[b * SSD_GROUPS + g]
            group_terms.append((b_g, prev, _dot_nt(b_g, c_g),
                                _dot_nt(prev.astype(BF16), c_g)))

    units = [(b, h) for b in range(bsz) for h in range(SSD_HEADS)]
    head_terms = []
    for b, h in units:
        dt_row, acum_col, acum_row = gate_terms[b]
        cb = group_terms[b * SSD_GROUPS + h // SSD_HEADS_PER_GROUP][2]
        r = DT_LANE + h - DT_ROW0
        lane = DT_LANE + h
        a_t = acum_row[r:r + 1, :]
        a_s = acum_col[:, lane:lane + 1]
        l_dec = jnp.exp(jnp.where(causal_t, a_t - a_s, -jnp.inf))
        x_h = xt_buf[b, h * SSD_HEAD_DIM:(h + 1) * SSD_HEAD_DIM, :]
        xc = x_h * dt_row[r:r + 1, :]
        a_last = a_t[:, CHUNK - 1:CHUNK]
        head_terms.append(((cb * l_dec).astype(BF16), x_h, xc.astype(BF16),
                           (xc * jnp.exp(a_last - a_t)).astype(BF16), jnp.exp(a_t), jnp.exp(a_last)))

    for i, (b, h) in enumerate(units):
        m_h, x_h, xc_b, _, exp_a, _ = head_terms[i]
        g, hh = divmod(h, SSD_HEADS_PER_GROUP)
        y_off = group_terms[b * SSD_GROUPS + g][3]
        hs = slice(h * SSD_HEAD_DIM, (h + 1) * SSD_HEAD_DIM)
        yt_buf[b, hs, :] = (_dot(xc_b, m_h)
                            + y_off[hh * SSD_HEAD_DIM:(hh + 1) * SSD_HEAD_DIM, :] * exp_a
                            + d_ref[layer, h] * x_h)

    for b in range(bsz):
        for g in range(SSD_GROUPS):
            st = b * SSD_GROUPS + g
            b_g, prev = group_terms[st][0:2]
            terms = head_terms[b * SSD_HEADS + g * SSD_HEADS_PER_GROUP:
                               b * SSD_HEADS + (g + 1) * SSD_HEADS_PER_GROUP]
            upd = _dot(jnp.concatenate([t[3] for t in terms], axis=0), b_g)
            for hh in range(SSD_HEADS_PER_GROUP):
                ps = slice(hh * SSD_HEAD_DIM, (hh + 1) * SSD_HEAD_DIM)
                st_ref[st, ps, :] = terms[hh][5] * prev[ps, :] + upd[ps, :]

    for b in range(bsz):
        for g in range(SSD_GROUPS):
            parts = []
            for blk in range(g * SSD_GROUP_WIDTH, (g + 1) * SSD_GROUP_WIDTH, 128):
                parts.append(yt_buf[b, blk:blk + 128, :].T * z_ref[b, :, blk:blk + 128].astype(F32))
            ssq = sum(jnp.sum(p * p, axis=1, keepdims=True) for p in parts)
            inv = lax.rsqrt(ssq * (1.0 / SSD_GROUP_WIDTH) + NORM_EPS)
            for i, p in enumerate(parts):
                cs = slice(g * SSD_GROUP_WIDTH + i * 128, g * SSD_GROUP_WIDTH + (i + 1) * 128)
                y_ref[b, :, Y_SSD + cs.start:Y_SSD + cs.stop] = (p * inv * norm_ref[:, cs]).astype(BF16)


def _ret_body(q_ref, k_ref, g_ref, cc_ref, ss_ref, vt_refs, norm_ref, y_ref, rt_ref, dmat_ref):
    bsz = len(vt_refs)
    log_gammas = [math.log1p(-2.0 ** (-5.0 - h)) for h in range(RET_HEADS)]

    @pl.when(pl.program_id(0) == 0)
    def _():
        rt_ref[...] = jnp.zeros(rt_ref.shape, F32)
        row, col = _chunk_iotas()
        rel = (col - row).astype(F32)
        for h in range(RET_HEADS):
            dmat_ref[h] = jnp.exp(jnp.where(row <= col, log_gammas[h] * rel, -jnp.inf))

    t_row = lax.broadcasted_iota(jnp.int32, (1, CHUNK), 1).astype(F32)
    half = RET_HEAD_DIM // 2
    scale = RET_HEAD_DIM ** -0.5

    roped = []
    for b in range(bsz):
        cc = cc_ref[b]
        ss = ss_ref[b]
        for h in range(RET_HEADS):
            sl = slice(h * RET_HEAD_DIM, (h + 1) * RET_HEAD_DIM)
            q = q_ref[b, :, sl].astype(F32)
            k = k_ref[b, :, sl].astype(F32)
            roped.append(((q * cc + pltpu.roll(q, half, 1) * ss).astype(BF16),
                          ((k * cc + pltpu.roll(k, half, 1) * ss) * scale).astype(BF16)))

    units = [(b, h) for b in range(bsz) for h in range(RET_HEADS)]
    xis = [jnp.exp(lg * (t_row + 1.0)) for lg in log_gammas]
    zetas = [jnp.exp(lg * (CHUNK - 1.0 - t_row)) for lg in log_gammas]
    scores, cross = [], []
    for st, (b, h) in enumerate(units):
        q, k = roped[st]
        scores.append((_dot_nt(k, q) * dmat_ref[h]).astype(BF16))
        cross.append(_dot_nt(rt_ref[st].astype(BF16), q) * xis[h])
    ys = []
    for st, (b, h) in enumerate(units):
        sl = slice(h * RET_HEAD_DIM, (h + 1) * RET_HEAD_DIM)
        _, k = roped[st]
        vt = vt_refs[b][sl, :]
        ys.append(_dot(vt, scores[st]) + cross[st])
        rt_ref[st] = (math.exp(log_gammas[h] * CHUNK) * rt_ref[st]
                      + _dot((vt.astype(F32) * zetas[h]).astype(BF16), k))
    for st, (b, h) in enumerate(units):
        sl = slice(h * RET_HEAD_DIM, (h + 1) * RET_HEAD_DIM)
        y = ys[st]
        yc = y - jnp.mean(y, axis=0, keepdims=True)
        yn = yc * lax.rsqrt(jnp.mean(yc * yc, axis=0, keepdims=True) + NORM_EPS)
        y_ref[b, :, Y_RET + sl.start:Y_RET + sl.stop] = (
            yn.T * norm_ref[:, sl] * g_ref[b, :, sl].astype(F32)).astype(BF16)


def _mixers_kernel(*refs, bsz, layer):
    it = iter(refs)
    take = lambda k=None: next(it) if k is None else [next(it) for _ in range(k)]
    d_ref = take()
    ml_q, ml_k, ml_o, ssd_x, ssd_b, ssd_c, ssd_z, ret_q, ret_k, ret_g, gates_ref, cc_ref, ss_ref = take(13)
    vml_refs, vret_refs, gatest_refs = take(bsz), take(bsz), take(bsz)
    brow_ref, bcol_ref, alog_row_ref, alog_col_ref, ml_norm_ref, ssd_norm_ref, ret_norm_ref = take(7)
    y_ref = take()
    ct_ref, n_ref, m_ref, st_ref, xt_buf, yt_buf, rt_ref, dmat_ref = take(8)
    _mlstm_body(ml_q, ml_k, ml_o, gates_ref, vml_refs, gatest_refs, brow_ref, bcol_ref, ml_norm_ref,
                y_ref, ct_ref, n_ref, m_ref)
    _ssd_body(d_ref, ssd_x, ssd_b, ssd_c, ssd_z, gates_ref, gatest_refs, brow_ref, bcol_ref,
              alog_row_ref, alog_col_ref, ssd_norm_ref, y_ref, st_ref, xt_buf, yt_buf, layer)
    _ret_body(ret_q, ret_k, ret_g, cc_ref, ss_ref, vret_refs, ret_norm_ref, y_ref, rt_ref, dmat_ref)


def _mixers(layer, d, proj, gates, cc, ss, vml_t, vret_t, gates_t, bias_row, bias_col, alog_row,
            alog_col, ml_norm, ssd_norm, ret_norm, bsz, nc):
    tok3 = lambda blk: (lambda c: (0, c, blk))
    seg = lambda col, width: pl.BlockSpec((bsz, CHUNK, width), tok3(col // width))
    return pl.pallas_call(
        functools.partial(_mixers_kernel, bsz=bsz, layer=layer),
        grid=(nc,),
        in_specs=[pl.BlockSpec(memory_space=pltpu.SMEM),
                  seg(COL_ML_Q, ML_WIDTH), seg(COL_ML_K, ML_WIDTH), seg(COL_ML_O, ML_WIDTH),
                  seg(COL_SSD_X, SSD_WIDTH), seg(COL_SSD_B, SSD_BC_WIDTH), seg(COL_SSD_C, SSD_BC_WIDTH),
                  seg(COL_SSD_Z, SSD_WIDTH),
                  seg(COL_RET_Q, RET_WIDTH), seg(COL_RET_K, RET_WIDTH), seg(COL_RET_G, RET_WIDTH),
                  seg(0, GATE_LANES), seg(0, RET_HEAD_DIM), seg(0, RET_HEAD_DIM)]
                 + _per_batch_specs(ML_WIDTH, bsz, nc) + _per_batch_specs(RET_WIDTH, bsz, nc)
                 + _per_batch_specs(GATE_ROWS, bsz, nc)
                 + [_layer_spec(layer, (1, GATE_LANES)),
                    _layer_spec(layer, (GATE_ROWS, 1)),
                    _layer_spec(layer, (1, GATE_LANES)),
                    _layer_spec(layer, (GATE_ROWS, 1)),
                    _layer_spec(layer, (1, ML_WIDTH)),
                    _layer_spec(layer, (1, SSD_WIDTH)),
                    _layer_spec(layer, (1, RET_WIDTH))],
        out_specs=pl.BlockSpec((bsz, CHUNK, MIX_WIDTH), tok3(0)),
        out_shape=jax.ShapeDtypeStruct((bsz, nc * CHUNK, MIX_WIDTH), BF16),
        scratch_shapes=[pltpu.VMEM((bsz * ML_HEADS, ML_HEAD_DIM, ML_HEAD_DIM), F32),
                        pltpu.VMEM((bsz * ML_HEADS, 1, ML_HEAD_DIM), F32),
                        pltpu.VMEM((bsz * ML_HEADS, 1, 128), F32),
                        pltpu.VMEM((bsz * SSD_GROUPS, SSD_GROUP_WIDTH, SSD_STATE), F32),
                        pltpu.VMEM((bsz, SSD_WIDTH, CHUNK), F32),
                        pltpu.VMEM((bsz, SSD_WIDTH, CHUNK), F32),
                        pltpu.VMEM((bsz * RET_HEADS, RET_HEAD_DIM, RET_HEAD_DIM), F32),
                        pltpu.VMEM((RET_HEADS, CHUNK, CHUNK), F32)],
        compiler_params=pltpu.CompilerParams(
            dimension_semantics=("arbitrary",), vmem_limit_bytes=VMEM_LIMIT),
        name="mixers",
    )(d, *([proj] * 10), gates, cc, ss, *([vml_t] * bsz), *([vret_t] * bsz), *([gates_t] * bsz),
      bias_row, bias_col, alog_row, alog_col, ml_norm, ssd_norm, ret_norm)


FFN_BLOCK = 256


def _out_ffn_kernel(x_ref, mix_ref, wo_ref, g_ref, wgu_ref, wd_ref, gfin_ref, o_ref, *, final_norm):
    x1 = x_ref[...] + _dot(mix_ref[...], wo_ref[...])
    h = (x1 * lax.rsqrt(jnp.mean(x1 * x1, axis=-1, keepdims=True) + NORM_EPS) * g_ref[...]).astype(BF16)
    acc = x1
    for blk in range(0, FFN_HIDDEN, FFN_BLOCK):
        gate = _dot(h, wgu_ref[:, blk:blk + FFN_BLOCK])
        up = _dot(h, wgu_ref[:, FFN_HIDDEN + blk:FFN_HIDDEN + blk + FFN_BLOCK])
        acc = acc + _dot((_silu(gate) * up).astype(BF16), wd_ref[blk:blk + FFN_BLOCK, :])
    if final_norm:
        acc = acc * lax.rsqrt(jnp.mean(acc * acc, axis=-1, keepdims=True) + NORM_EPS) * gfin_ref[...]
    o_ref[...] = acc


def _out_ffn(layer, x, mix, w_out, g, w_gu, w_down, g_final, final_norm):
    n = x.shape[0]
    tm = min(n, 512)
    assert n % tm == 0
    rows = lambda i: (i, 0)
    const = lambda i: (0, 0)
    resident = functools.partial(_layer_spec, layer, pipeline_mode=pl.Buffered(1))
    return pl.pallas_call(
        functools.partial(_out_ffn_kernel, final_norm=final_norm),
        grid=(n // tm,),
        in_specs=[pl.BlockSpec((tm, D_MODEL), rows),
                  pl.BlockSpec((tm, MIX_WIDTH), rows),
                  resident((MIX_WIDTH, D_MODEL)),
                  _layer_spec(layer, (1, D_MODEL)),
                  resident((D_MODEL, 2 * FFN_HIDDEN)),
                  resident((FFN_HIDDEN, D_MODEL)),
                  pl.BlockSpec((1, D_MODEL), const)],
        out_specs=pl.BlockSpec((tm, D_MODEL), rows),
        out_shape=jax.ShapeDtypeStruct((n, D_MODEL), F32),
        compiler_params=pltpu.CompilerParams(
            dimension_semantics=("parallel",), vmem_limit_bytes=VMEM_LIMIT),
        name="out_ffn",
    )(x, mix, w_out, g, w_gu, w_down, g_final)


def _gate_row(v, offset):
    depth, k = v.shape
    return jnp.zeros((depth, 1, GATE_LANES), F32).at[:, 0, offset:offset + k].set(v)


def _gate_col(v, offset):
    depth, k = v.shape
    return jnp.zeros((depth, GATE_ROWS, 1), F32).at[:, offset:offset + k, 0].set(v)


def kernel(x, positions, norm_mix, w_in, ml_conv_w, ml_conv_b, ml_gate_bias, ml_norm,
           ssd_conv_w, ssd_conv_b, ssd_dt_bias, ssd_a_log, ssd_d, ssd_norm, ret_norm,
           w_out, norm_ffn, w_gate_up, w_down, norm_final):
    bsz, seq, _ = x.shape
    depth = w_in.shape[0]
    n = bsz * seq
    nc = seq // CHUNK
    assert seq % CHUNK == 0 and x.shape[2] == D_MODEL and w_in.shape[2] == ML_IN + SSD_IN + RET_IN

    ml0, ssd0, ret0 = 0, ML_IN, ML_IN + SSD_IN
    w_in_b = w_in.astype(BF16)
    seg = lambda start, width: w_in_b[:, :, start:start + width]
    ml_q, ml_k = seg(ml0, ML_WIDTH), seg(ml0 + ML_WIDTH, ML_WIDTH)
    ml_v, ml_o = seg(ml0 + 2 * ML_WIDTH, ML_WIDTH), seg(ml0 + 3 * ML_WIDTH, ML_WIDTH)
    ml_if = seg(ml0 + 4 * ML_WIDTH, 2 * ML_HEADS)
    ssd_z, ssd_x = seg(ssd0, SSD_WIDTH), seg(ssd0 + SSD_WIDTH, SSD_WIDTH)
    ssd_bc = seg(ssd0 + 2 * SSD_WIDTH, 2 * SSD_BC_WIDTH)
    ssd_dt = seg(ssd0 + SSD_WIDTH + SSD_CONV_DIM, SSD_HEADS)
    ret_qk, ret_v = seg(ret0, 2 * RET_WIDTH), seg(ret0 + 2 * RET_WIDTH, RET_WIDTH)
    ret_g = seg(ret0 + 3 * RET_WIDTH, RET_WIDTH)
    w_wide = jnp.concatenate([ret_qk, ret_g, ml_q, ml_k, ssd_x, ml_o, ssd_z, ssd_bc], axis=-1)
    w_vml_t = jnp.swapaxes(ml_v, 1, 2)
    w_vret_t = jnp.swapaxes(ret_v, 1, 2)
    w_gate_cols = jnp.concatenate([ml_if, ssd_dt], axis=-1)
    n_gate = w_gate_cols.shape[-1]
    w_nar = jnp.pad(w_gate_cols, ((0, 0), (0, 0), (0, GATE_LANES - n_gate)))
    w_nar_t = jnp.pad(jnp.swapaxes(w_gate_cols, 1, 2), ((0, 0), (0, GATE_ROWS - n_gate), (0, 0)))
    conv_w = jnp.concatenate([ml_conv_w, ssd_conv_w], axis=-1)
    conv_b = jnp.concatenate([ml_conv_b, ssd_conv_b], axis=-1)[:, None, :]
    w_out_b = w_out.astype(BF16)
    w_gu_b = w_gate_up.astype(BF16)
    w_down_b = w_down.astype(BF16)

    gate_bias = jnp.concatenate([ml_gate_bias, ssd_dt_bias], axis=-1)
    bias_row, bias_col = _gate_row(gate_bias, 0), _gate_col(gate_bias, 0)
    alog_row, alog_col = _gate_row(ssd_a_log, DT_LANE), _gate_col(ssd_a_log, DT_LANE)
    row3 = lambda v: v[:, None, :]
    norm_mix3, norm_ffn3 = row3(norm_mix), row3(norm_ffn)
    ml_norm3, ssd_norm3, ret_norm3 = row3(ml_norm), row3(ssd_norm), row3(ret_norm)
    g_final = norm_final.reshape(1, D_MODEL)

    cc, ss = _rope_tables(positions)
    cc = cc.reshape(bsz, seq, RET_HEAD_DIM)
    ss = ss.reshape(bsz, seq, RET_HEAD_DIM)
    xf = x.reshape(n, D_MODEL)
    for l in range(depth):
        proj, vml_t, vret_t, gates, gates_t = _inproj(
            l, xf, norm_mix3, w_wide, w_vml_t, w_vret_t, w_nar, w_nar_t, conv_w, conv_b, seq)
        proj = proj.reshape(bsz, seq, WIDE_WIDTH)
        gates = gates.reshape(bsz, seq, GATE_LANES)
        mix = _mixers(l, ssd_d, proj, gates, cc, ss, vml_t, vret_t, gates_t, bias_row, bias_col,
                      alog_row, alog_col, ml_norm3, ssd_norm3, ret_norm3, bsz, nc)
        xf = _out_ffn(l, xf, mix.reshape(n, MIX_WIDTH), w_out_b, norm_ffn3, w_gu_b, w_down_b,
                      g_final, final_norm=(l == depth - 1))
    return xf.reshape(bsz, seq, D_MODEL)
```

```python
import functools
import math

import jax
import jax.numpy as jnp
from jax import lax
from jax.experimental import pallas as pl
from jax.experimental.pallas import tpu as pltpu

F32 = jnp.float32
BF16 = jnp.bfloat16

D_MODEL = 1024
CHUNK = 128
CONV_K = 4
NORM_EPS = 1e-6
ML_HEADS = 6
ML_HEAD_DIM = 128
ML_WIDTH = ML_HEADS * ML_HEAD_DIM
SSD_HEADS = 12
SSD_HEAD_DIM = 64
SSD_WIDTH = SSD_HEADS * SSD_HEAD_DIM
SSD_GROUPS = 2
SSD_HEADS_PER_GROUP = SSD_HEADS // SSD_GROUPS
SSD_GROUP_WIDTH = SSD_HEADS_PER_GROUP * SSD_HEAD_DIM
SSD_STATE = 128
SSD_BC_WIDTH = SSD_GROUPS * SSD_STATE
SSD_CONV_DIM = SSD_WIDTH + 2 * SSD_BC_WIDTH
RET_HEADS = 4
RET_HEAD_DIM = 128
RET_WIDTH = RET_HEADS * RET_HEAD_DIM
ROPE_BASE = 10000.0
MIX_WIDTH = ML_WIDTH + SSD_WIDTH + RET_WIDTH
Y_ML, Y_SSD, Y_RET = 0, ML_WIDTH, ML_WIDTH + SSD_WIDTH
ML_IN = 4 * ML_WIDTH + 2 * ML_HEADS
SSD_IN = SSD_WIDTH + SSD_CONV_DIM + SSD_HEADS
RET_IN = 4 * RET_WIDTH
FFN_HIDDEN = 2816

SEGMENTS = (
    ("ret_q", RET_WIDTH, "plain"),
    ("ret_k", RET_WIDTH, "plain"),
    ("ret_g", RET_WIDTH, "silu"),
    ("ml_q", ML_WIDTH, "conv"),
    ("ml_k", ML_WIDTH, "conv_scaled"),
    ("ssd_x", SSD_WIDTH, "conv"),
    ("ml_o", ML_WIDTH, "sigmoid"),
    ("ssd_z", SSD_WIDTH, "silu"),
    ("ssd_b", SSD_BC_WIDTH, "conv"),
    ("ssd_c", SSD_BC_WIDTH, "conv"),
)
SEG_INDEX = {name: i for i, (name, _, _) in enumerate(SEGMENTS)}
WIDE_WIDTH = sum(w for _, w, _ in SEGMENTS)
PROJ_BLOCK = 256
CONV_WIDTH = 2 * ML_WIDTH + SSD_CONV_DIM

GATE_LANES = 128
GATE_ROWS = 32
I_LANE = 0
F_LANE = ML_HEADS
DT_LANE = 2 * ML_HEADS
ML_GATE_ROWS = 16
DT_ROW0 = 8
TAIL = 8

VMEM_LIMIT = 56 * 1024 * 1024

_NT = (((1,), (1,)), ((), ()))


def _sigmoid(x):
    return 1.0 / (1.0 + jnp.exp2(x * (-1.0 / math.log(2.0))))


def _silu(x):
    return x * _sigmoid(x)


def _log_sigmoid(x):
    return jnp.minimum(x, 0.0) - jnp.log1p(jnp.exp(-jnp.abs(x)))


def _softplus(x):
    return jnp.maximum(x, 0.0) + jnp.log1p(jnp.exp(-jnp.abs(x)))


def _dot(a, b):
    return jnp.dot(a, b, preferred_element_type=F32)


def _dot_nt(a, b):
    return lax.dot_general(a, b, _NT, preferred_element_type=F32)


def _dot_exact(a, b):
    return jnp.dot(a, b, preferred_element_type=F32, precision=lax.Precision.HIGHEST)


def _row_vec_dot(vec, mat_nt=None, mat=None):
    rows = 16
    v = jnp.broadcast_to(vec, (rows, vec.shape[1]))
    hi = v.astype(BF16).astype(F32)
    r = lax.broadcasted_iota(jnp.int32, v.shape, 0)
    lhs = jnp.where(r == 0, hi, jnp.where(r == 1, v - hi, 0.0)).astype(BF16)
    out = _dot_nt(lhs, mat_nt) if mat_nt is not None else _dot(lhs, mat)
    return jnp.sum(out, axis=0, keepdims=True)


def _chunk_iotas():
    row = lax.broadcasted_iota(jnp.int32, (CHUNK, CHUNK), 0)
    col = lax.broadcasted_iota(jnp.int32, (CHUNK, CHUNK), 1)
    return row, col


def _rope_kernel(pos_ref, invf_ref, cc_ref, ss_ref):
    ang = pos_ref[...].astype(F32) * invf_ref[...]
    lane = lax.broadcasted_iota(jnp.int32, ang.shape, 1)
    sin = jnp.sin(ang)
    cc_ref[...] = jnp.cos(ang)
    ss_ref[...] = jnp.where(lane < RET_HEAD_DIM // 2, -sin, sin)


def _rope_tables(positions):
    n = positions.size
    tm = min(n, 2048)
    assert n % tm == 0
    half = RET_HEAD_DIM // 2
    inv_freq = ROPE_BASE ** (-jnp.arange(0, RET_HEAD_DIM, 2, dtype=F32) / RET_HEAD_DIM)
    invf = jnp.concatenate([inv_freq, inv_freq]).reshape(1, 2 * half)
    return pl.pallas_call(
        _rope_kernel,
        grid=(n // tm,),
        in_specs=[pl.BlockSpec((tm, 1), lambda i: (i, 0)),
                  pl.BlockSpec((1, RET_HEAD_DIM), lambda i: (0, 0))],
        out_specs=[pl.BlockSpec((tm, RET_HEAD_DIM), lambda i: (i, 0))] * 2,
        out_shape=[jax.ShapeDtypeStruct((n, RET_HEAD_DIM), F32)] * 2,
        name="rope_tables",
    )(positions.reshape(n, 1), invf)


def _inproj_kernel(*refs, tiles_per_seq):
    x_ref, g_ref, w_ref, wvml_ref, wvret_ref, wn_ref, wnt_ref, cw_ref, cb_ref = refs[0:9]
    seg_refs = refs[9:9 + len(SEGMENTS)]
    vml_ref, vret_ref, gates_ref, gatest_ref, h_ref, cbuf = refs[9 + len(SEGMENTS):]
    tm = x_ref.shape[0]

    @pl.when(pl.program_id(0) % tiles_per_seq == 0)
    def _():
        cbuf[0:TAIL, :] = jnp.zeros((TAIL, cbuf.shape[1]), F32)

    def store_chunks(ref, rows, value):
        for c in range(tm // CHUNK):
            ref[c, rows, :] = value[:, c * CHUNK:(c + 1) * CHUNK]

    x = x_ref[...]
    y = x * lax.rsqrt(jnp.mean(x * x, axis=-1, keepdims=True) + NORM_EPS) * g_ref[...]
    h_ref[...] = y.astype(BF16)
    gates_ref[...] = _dot(h_ref[...], wn_ref[...])
    store_chunks(gatest_ref, slice(0, GATE_ROWS), _dot_nt(wnt_ref[...], h_ref[...]))

    def conv_epilogue(acc, o_ref, cols, cc, scaled):
        cbuf[TAIL:TAIL + tm, cc] = acc
        out = cb_ref[:, cc] + cw_ref[CONV_K - 1:CONV_K, cc] * acc
        for j in range(1, CONV_K):
            out = out + cw_ref[CONV_K - 1 - j:CONV_K - j, cc] * cbuf[TAIL - j:TAIL - j + tm, cc]
        cbuf[0:TAIL, cc] = cbuf[tm:tm + TAIL, cc]
        out = _silu(out)
        if scaled:
            out = out * (ML_HEAD_DIM ** -0.5)
        o_ref[:, cols] = out.astype(BF16)

    def plain_epilogue(acc, o_ref, cols, act):
        o_ref[:, cols] = (acc if act is None else act(acc)).astype(BF16)

    def t_epilogue(acc, ref, rows):
        store_chunks(ref, rows, acc.astype(BF16))

    heavy, light = [], []
    conv_col = 0
    start = 0
    for o_ref, (_, width, kind) in zip(seg_refs, SEGMENTS):
        for blk in range(0, width, PROJ_BLOCK):
            cols = slice(blk, blk + PROJ_BLOCK)
            wcols = slice(start + blk, start + blk + PROJ_BLOCK)
            mm = functools.partial(lambda c: _dot(h_ref[...], w_ref[:, c]), wcols)
            if kind in ("conv", "conv_scaled"):
                cc = slice(conv_col, conv_col + PROJ_BLOCK)
                conv_col += PROJ_BLOCK
                heavy.append((mm, functools.partial(conv_epilogue, o_ref=o_ref, cols=cols, cc=cc,
                                                    scaled=(kind == "conv_scaled"))))
            else:
                act = {"silu": _silu, "sigmoid": _sigmoid, "plain": None}[kind]
                light.append((mm, functools.partial(plain_epilogue, o_ref=o_ref, cols=cols, act=act)))
        start += width
    for w_t_ref, out_ref, width in ((wvml_ref, vml_ref, ML_WIDTH), (wvret_ref, vret_ref, RET_WIDTH)):
        for blk in range(0, width, PROJ_BLOCK):
            rows = slice(blk, blk + PROJ_BLOCK)
            mm = functools.partial(lambda r, w: _dot_nt(w[r, :], h_ref[...]), rows, w_t_ref)
            light.append((mm, functools.partial(t_epilogue, ref=out_ref, rows=rows)))

    jobs = []
    while heavy or light:
        if light:
            jobs.append(light.pop(0))
        if heavy:
            jobs.append(heavy.pop(0))
    acc = jobs[0][0]()
    for i, (_, epilogue) in enumerate(jobs):
        nxt = jobs[i + 1][0]() if i + 1 < len(jobs) else None
        epilogue(acc)
        acc = nxt


def _layer_spec(layer, shape, **kwargs):
    return pl.BlockSpec((None,) + tuple(shape), lambda *_: (layer,) + (0,) * len(shape), **kwargs)


def _inproj(layer, x, g, w_wide, w_vml_t, w_vret_t, w_nar, w_nar_t, conv_w, conv_b, seq):
    n = x.shape[0]
    tm = min(seq, 512)
    assert seq % tm == 0 and n % seq == 0 and tm % CHUNK == 0
    rows = lambda i: (i, 0)
    chunks = lambda i: (i, 0, 0)
    cpt = tm // CHUNK
    resident = functools.partial(_layer_spec, layer, pipeline_mode=pl.Buffered(1))
    return pl.pallas_call(
        functools.partial(_inproj_kernel, tiles_per_seq=seq // tm),
        grid=(n // tm,),
        in_specs=[pl.BlockSpec((tm, D_MODEL), rows),
                  _layer_spec(layer, (1, D_MODEL)),
                  resident((D_MODEL, WIDE_WIDTH)),
                  resident((ML_WIDTH, D_MODEL)),
                  resident((RET_WIDTH, D_MODEL)),
                  resident((D_MODEL, GATE_LANES)),
                  resident((GATE_ROWS, D_MODEL)),
                  _layer_spec(layer, (CONV_K, CONV_WIDTH)),
                  _layer_spec(layer, (1, CONV_WIDTH))],
        out_specs=[pl.BlockSpec((tm, width), rows) for _, width, _ in SEGMENTS]
                  + [pl.BlockSpec((cpt, ML_WIDTH, CHUNK), chunks),
                     pl.BlockSpec((cpt, RET_WIDTH, CHUNK), chunks),
                     pl.BlockSpec((tm, GATE_LANES), rows),
                     pl.BlockSpec((cpt, GATE_ROWS, CHUNK), chunks)],
        out_shape=[jax.ShapeDtypeStruct((n, width), BF16) for _, width, _ in SEGMENTS]
                  + [jax.ShapeDtypeStruct((n // CHUNK, ML_WIDTH, CHUNK), BF16),
                     jax.ShapeDtypeStruct((n // CHUNK, RET_WIDTH, CHUNK), BF16),
                     jax.ShapeDtypeStruct((n, GATE_LANES), F32),
                     jax.ShapeDtypeStruct((n // CHUNK, GATE_ROWS, CHUNK), F32)],
        scratch_shapes=[pltpu.VMEM((tm, D_MODEL), BF16),
                        pltpu.VMEM((TAIL + tm, CONV_WIDTH), F32)],
        compiler_params=pltpu.CompilerParams(
            dimension_semantics=("arbitrary",), vmem_limit_bytes=VMEM_LIMIT),
        name="inproj",
    )(x, g, w_wide, w_vml_t, w_vret_t, w_nar, w_nar_t, conv_w, conv_b)


def _mlstm_body(q_ref, k_ref, o_ref, gates_ref, vt_refs, gatest_refs, brow_ref, bcol_ref, norm_ref,
                y_ref, ct_ref, n_ref, m_ref):
    bsz = len(vt_refs)

    @pl.when(pl.program_id(0) == 0)
    def _():
        ct_ref[...] = jnp.zeros(ct_ref.shape, F32)
        n_ref[...] = jnp.zeros(n_ref.shape, F32)
        m_ref[...] = jnp.zeros(m_ref.shape, F32)

    row, col = _chunk_iotas()
    causal_t = row <= col
    tri = (col <= row).astype(F32)
    tri_t = causal_t.astype(F32)

    gate_terms = []
    for b in range(bsz):
        g_col = gates_ref[b] + brow_ref[...]
        g_row = gatest_refs[b][0:ML_GATE_ROWS, :] + bcol_ref[0:ML_GATE_ROWS, :]
        b_col = _dot_exact(tri, _log_sigmoid(g_col))
        b_row = _dot_exact(_log_sigmoid(g_row), tri_t)
        gate_terms.append((g_row, b_row, g_col - pltpu.roll(b_col, GATE_LANES - F_LANE, 1)))

    units = [(b, h) for b in range(bsz) for h in range(ML_HEADS)]
    head = lambda h: slice(h * ML_HEAD_DIM, (h + 1) * ML_HEAD_DIM)

    stage1 = []
    for st, (b, h) in enumerate(units):
        g_row, b_row, cs_all = gate_terms[b]
        q = q_ref[b, :, head(h)]
        k = k_ref[b, :, head(h)]
        m_prev = m_ref[st][:, 0:1]
        b_t = b_row[F_LANE + h:F_LANE + h + 1, :]
        cs = cs_all[:, h:h + 1]
        log_d = jnp.where(causal_t, b_t + cs, -jnp.inf)
        a = b_t + m_prev
        m_t = jnp.maximum(a, jnp.max(log_d, axis=0, keepdims=True))
        inter = jnp.exp(a - m_t)
        scores = _dot_nt(k, q) * jnp.exp(log_d - m_t)
        den = jnp.sum(scores, axis=0, keepdims=True) + inter * _row_vec_dot(n_ref[st], mat_nt=q)
        scale = 1.0 / jnp.maximum(jnp.abs(den), jnp.exp(-m_t))
        cross = _dot_nt(ct_ref[st].astype(BF16), q) * (inter * scale)
        stage1.append((scores.astype(BF16), cross, scale))

    outs = []
    for st, (b, h) in enumerate(units):
        scores, cross, scale = stage1[st]
        out = _dot(vt_refs[b][head(h), :], scores) * scale + cross
        outs.append(out * lax.rsqrt(jnp.mean(out * out, axis=0, keepdims=True) + NORM_EPS))

    for st, (b, h) in enumerate(units):
        g_row, b_row, _ = gate_terms[b]
        k = k_ref[b, :, head(h)]
        vt = vt_refs[b][head(h), :]
        m_prev = m_ref[st][:, 0:1]
        b_t = b_row[F_LANE + h:F_LANE + h + 1, :]
        i_t = g_row[I_LANE + h:I_LANE + h + 1, :]
        b_last = b_t[:, CHUNK - 1:CHUNK]
        w_state = b_last - b_t + i_t
        m_new = jnp.maximum(b_last + m_prev, jnp.max(w_state, axis=1, keepdims=True))
        a_old = jnp.exp(b_last + m_prev - m_new)
        e = jnp.exp(w_state - m_new)
        ct_ref[st] = a_old * ct_ref[st] + _dot((vt.astype(F32) * e).astype(BF16), k)
        n_ref[st] = a_old * n_ref[st] + _row_vec_dot(e, mat=k)
        m_ref[st] = jnp.broadcast_to(m_new, m_ref.shape[1:])

    for st, (b, h) in enumerate(units):
        y_ref[b, :, Y_ML + h * ML_HEAD_DIM:Y_ML + (h + 1) * ML_HEAD_DIM] = (
            outs[st].T * norm_ref[:, head(h)] * o_ref[b, :, head(h)].astype(F32)).astype(BF16)


def _per_batch_specs(rows, bsz, nc):
    return [pl.BlockSpec((None, rows, CHUNK), functools.partial(lambda b, c: (b * nc + c, 0, 0), b))
            for b in range(bsz)]


def _ssd_body(d_ref, x_ref, b_ref, c_ref, z_ref, gates_ref, gatest_refs, brow_ref, bcol_ref,
              alog_row_ref, alog_col_ref, norm_ref, y_ref, st_ref, xt_buf, yt_buf, layer):
    bsz = len(gatest_refs)

    @pl.when(pl.program_id(0) == 0)
    def _():
        st_ref[...] = jnp.zeros(st_ref.shape, F32)

    row, col = _chunk_iotas()
    causal_t = row <= col
    tri = (col <= row).astype(F32)
    tri_t = causal_t.astype(F32)
    rows16 = slice(DT_ROW0, DT_ROW0 + 16)
    a_neg_row = -jnp.exp(alog_row_ref[...])
    a_neg_col = -jnp.exp(alog_col_ref[rows16, :])

    gate_terms = []
    group_terms = []
    for b in range(bsz):
        dt_col = _softplus(gates_ref[b] + brow_ref[...])
        dt_row = _softplus(gatest_refs[b][rows16, :] + bcol_ref[rows16, :])
        acum_col = _dot_exact(tri, dt_col * a_neg_row)
        acum_row = _dot_exact(dt_row * a_neg_col, tri_t)
        gate_terms.append((dt_row, acum_col, acum_row))
        for blk in range(0, SSD_WIDTH, 128):
            xt_buf[b, blk:blk + 128, :] = x_ref[b, :, blk:blk + 128].astype(F32).T
        for g in range(SSD_GROUPS):
            gs = slice(g * SSD_STATE, (g + 1) * SSD_STATE)
            b_g = b_ref[b, :, gs]
            c_g = c_ref[b, :, gs]
            prev = st_ref[b * SSD_GROUPS + g]
            group_terms.append((b_g, prev, _dot_nt(b_g, c_g),
                                _dot_nt(prev.astype(BF16), c_g)))

    units = [(b, h) for b in range(bsz) for h in range(SSD_HEADS)]
    head_terms = []
    for b, h in units:
        dt_row, acum_col, acum_row = gate_terms[b]
        cb = group_terms[b * SSD_GROUPS + h // SSD_HEADS_PER_GROUP][2]
        r = DT_LANE + h - DT_ROW0
        lane = DT_LANE + h
        a_t = acum_row[r:r + 1, :]
        a_s = acum_col[:, lane:lane + 1]
        l_dec = jnp.exp(jnp.where(causal_t, a_t - a_s, -jnp.inf))
        x_h = xt_buf[b, h * SSD_HEAD_DIM:(h + 1) * SSD_HEAD_DIM, :]
        xc = x_h * dt_row[r:r + 1, :]
        a_last = a_t[:, CHUNK - 1:CHUNK]
        head_terms.append(((cb * l_dec).astype(BF16), x_h, xc.astype(BF16),
                           (xc * jnp.exp(a_last - a_t)).astype(BF16), jnp.exp(a_t), jnp.exp(a_last)))

    for i, (b, h) in enumerate(units):
        m_h, x_h, xc_b, _, exp_a, _ = head_terms[i]
        g, hh = divmod(h, SSD_HEADS_PER_GROUP)
        y_off = group_terms[b * SSD_GROUPS + g][3]
        hs = slice(h * SSD_HEAD_DIM, (h + 1) * SSD_HEAD_DIM)
        yt_buf[b, hs, :] = (_dot(xc_b, m_h)
                            + y_off[hh * SSD_HEAD_DIM:(hh + 1) * SSD_HEAD_DIM, :] * exp_a
                            + d_ref[layer, h] * x_h)

    for b in range(bsz):
        for g in range(SSD_GROUPS):
            st = b * SSD_GROUPS + g
            b_g, prev = group_terms[st][0:2]
            terms = head_terms[b * SSD_HEADS + g * SSD_HEADS_PER_GROUP:
                               b * SSD_HEADS + (g + 1) * SSD_HEADS_PER_GROUP]
            upd = _dot(jnp.concatenate([t[3] for t in terms], axis=0), b_g)
            for hh in range(SSD_HEADS_PER_GROUP):
                ps = slice(hh * SSD_HEAD_DIM, (hh + 1) * SSD_HEAD_DIM)
                st_ref[st, ps, :] = terms[hh][5] * prev[ps, :] + upd[ps, :]

    for b in range(bsz):
        for g in range(SSD_GROUPS):
            parts = []
            for blk in range(g * SSD_GROUP_WIDTH, (g + 1) * SSD_GROUP_WIDTH, 128):
                parts.append(yt_buf[b, blk:blk + 128, :].T * z_ref[b, :, blk:blk + 128].astype(F32))
            ssq = sum(jnp.sum(p * p, axis=1, keepdims=True) for p in parts)
            inv = lax.rsqrt(ssq * (1.0 / SSD_GROUP_WIDTH) + NORM_EPS)
            for i, p in enumerate(parts):
                cs = slice(g * SSD_GROUP_WIDTH + i * 128, g * SSD_GROUP_WIDTH + (i + 1) * 128)
                y_ref[b, :, Y_SSD + cs.start:Y_SSD + cs.stop] = (p * inv * norm_ref[:, cs]).astype(BF16)


def _ret_body(q_ref, k_ref, g_ref, cc_ref, ss_ref, vt_refs, norm_ref, y_ref, rt_ref, dmat_ref):
    bsz = len(vt_refs)
    log_gammas = [math.log1p(-2.0 ** (-5.0 - h)) for h in range(RET_HEADS)]

    @pl.when(pl.program_id(0) == 0)
    def _():
        rt_ref[...] = jnp.zeros(rt_ref.shape, F32)
        row, col = _chunk_iotas()
        rel = (col - row).astype(F32)
        for h in range(RET_HEADS):
            dmat_ref[h] = jnp.exp(jnp.where(row <= col, log_gammas[h] * rel, -jnp.inf))

    t_row = lax.broadcasted_iota(jnp.int32, (1, CHUNK), 1).astype(F32)
    half = RET_HEAD_DIM // 2
    scale = RET_HEAD_DIM ** -0.5

    roped = []
    for b in range(bsz):
        cc = cc_ref[b]
        ss = ss_ref[b]
        for h in range(RET_HEADS):
            sl = slice(h * RET_HEAD_DIM, (h + 1) * RET_HEAD_DIM)
            q = q_ref[b, :, sl].astype(F32)
            k = k_ref[b, :, sl].astype(F32)
            roped.append(((q * cc + pltpu.roll(q, half, 1) * ss).astype(BF16),
                          ((k * cc + pltpu.roll(k, half, 1) * ss) * scale).astype(BF16)))

    units = [(b, h) for b in range(bsz) for h in range(RET_HEADS)]
    xis = [jnp.exp(lg * (t_row + 1.0)) for lg in log_gammas]
    zetas = [jnp.exp(lg * (CHUNK - 1.0 - t_row)) for lg in log_gammas]
    scores, cross = [], []
    for st, (b, h) in enumerate(units):
        q, k = roped[st]
        scores.append((_dot_nt(k, q) * dmat_ref[h]).astype(BF16))
        cross.append(_dot_nt(rt_ref[st].astype(BF16), q) * xis[h])
    ys = []
    for st, (b, h) in enumerate(units):
        sl = slice(h * RET_HEAD_DIM, (h + 1) * RET_HEAD_DIM)
        _, k = roped[st]
        vt = vt_refs[b][sl, :]
        ys.append(_dot(vt, scores[st]) + cross[st])
        rt_ref[st] = (math.exp(log_gammas[h] * CHUNK) * rt_ref[st]
                      + _dot((vt.astype(F32) * zetas[h]).astype(BF16), k))
    for st, (b, h) in enumerate(units):
        sl = slice(h * RET_HEAD_DIM, (h + 1) * RET_HEAD_DIM)
        y = ys[st]
        yc = y - jnp.mean(y, axis=0, keepdims=True)
        yn = yc * lax.rsqrt(jnp.mean(yc * yc, axis=0, keepdims=True) + NORM_EPS)
        y_ref[b, :, Y_RET + sl.start:Y_RET + sl.stop] = (
            yn.T * norm_ref[:, sl] * g_ref[b, :, sl].astype(F32)).astype(BF16)


def _mixers_kernel(*refs, bsz, layer):
    it = iter(refs)
    take = lambda k=None: next(it) if k is None else [next(it) for _ in range(k)]
    d_ref = take()
    ml_q, ml_k, ml_o, ssd_x, ssd_b, ssd_c, ssd_z, ret_q, ret_k, ret_g, gates_ref, cc_ref, ss_ref = take(13)
    vml_refs, vret_refs, gatest_refs = take(bsz), take(bsz), take(bsz)
    brow_ref, bcol_ref, alog_row_ref, alog_col_ref, ml_norm_ref, ssd_norm_ref, ret_norm_ref = take(7)
    y_ref = take()
    ct_ref, n_ref, m_ref, st_ref, xt_buf, yt_buf, rt_ref, dmat_ref = take(8)
    _mlstm_body(ml_q, ml_k, ml_o, gates_ref, vml_refs, gatest_refs, brow_ref, bcol_ref, ml_norm_ref,
                y_ref, ct_ref, n_ref, m_ref)
    _ssd_body(d_ref, ssd_x, ssd_b, ssd_c, ssd_z, gates_ref, gatest_refs, brow_ref, bcol_ref,
              alog_row_ref, alog_col_ref, ssd_norm_ref, y_ref, st_ref, xt_buf, yt_buf, layer)
    _ret_body(ret_q, ret_k, ret_g, cc_ref, ss_ref, vret_refs, ret_norm_ref, y_ref, rt_ref, dmat_ref)


def _mixers(layer, d, segs, gates, cc, ss, vml_t, vret_t, gates_t, bias_row, bias_col, alog_row,
            alog_col, ml_norm, ssd_norm, ret_norm, bsz, nc):
    tok3 = lambda c: (0, c, 0)
    names = ("ml_q", "ml_k", "ml_o", "ssd_x", "ssd_b", "ssd_c", "ssd_z", "ret_q", "ret_k", "ret_g")
    seg_args = [segs[SEG_INDEX[name]] for name in names]
    return pl.pallas_call(
        functools.partial(_mixers_kernel, bsz=bsz, layer=layer),
        grid=(nc,),
        in_specs=[pl.BlockSpec(memory_space=pltpu.SMEM)]
                 + [pl.BlockSpec((bsz, CHUNK, a.shape[-1]), tok3) for a in seg_args + [gates, cc, ss]]
                 + _per_batch_specs(ML_WIDTH, bsz, nc) + _per_batch_specs(RET_WIDTH, bsz, nc)
                 + _per_batch_specs(GATE_ROWS, bsz, nc)
                 + [_layer_spec(layer, (1, GATE_LANES)),
                    _layer_spec(layer, (GATE_ROWS, 1)),
                    _layer_spec(layer, (1, GATE_LANES)),
                    _layer_spec(layer, (GATE_ROWS, 1)),
                    _layer_spec(layer, (1, ML_WIDTH)),
                    _layer_spec(layer, (1, SSD_WIDTH)),
                    _layer_spec(layer, (1, RET_WIDTH))],
        out_specs=pl.BlockSpec((bsz, CHUNK, MIX_WIDTH), tok3),
        out_shape=jax.ShapeDtypeStruct((bsz, nc * CHUNK, MIX_WIDTH), BF16),
        scratch_shapes=[pltpu.VMEM((bsz * ML_HEADS, ML_HEAD_DIM, ML_HEAD_DIM), F32),
                        pltpu.VMEM((bsz * ML_HEADS, 1, ML_HEAD_DIM), F32),
                        pltpu.VMEM((bsz * ML_HEADS, 1, 128), F32),
                        pltpu.VMEM((bsz * SSD_GROUPS, SSD_GROUP_WIDTH, SSD_STATE), F32),
                        pltpu.VMEM((bsz, SSD_WIDTH, CHUNK), F32),
                        pltpu.VMEM((bsz, SSD_WIDTH, CHUNK), F32),
                        pltpu.VMEM((bsz * RET_HEADS, RET_HEAD_DIM, RET_HEAD_DIM), F32),
                        pltpu.VMEM((RET_HEADS, CHUNK, CHUNK), F32)],
        compiler_params=pltpu.CompilerParams(
            dimension_semantics=("arbitrary",), vmem_limit_bytes=VMEM_LIMIT),
        name="mixers",
    )(d, *seg_args, gates, cc, ss, *([vml_t] * bsz), *([vret_t] * bsz), *([gates_t] * bsz),
      bias_row, bias_col, alog_row, alog_col, ml_norm, ssd_norm, ret_norm)


FFN_BLOCK = 256


def _out_ffn_kernel(x_ref, mix_ref, wo_ref, g_ref, wgu_ref, wd_ref, gfin_ref, o_ref, *, final_norm):
    x1 = x_ref[...] + _dot(mix_ref[...], wo_ref[...])
    h = (x1 * lax.rsqrt(jnp.mean(x1 * x1, axis=-1, keepdims=True) + NORM_EPS) * g_ref[...]).astype(BF16)
    acc = x1
    for blk in range(0, FFN_HIDDEN, FFN_BLOCK):
        gate = _dot(h, wgu_ref[:, blk:blk + FFN_BLOCK])
        up = _dot(h, wgu_ref[:, FFN_HIDDEN + blk:FFN_HIDDEN + blk + FFN_BLOCK])
        acc = acc + _dot((_silu(gate) * up).astype(BF16), wd_ref[blk:blk + FFN_BLOCK, :])
    if final_norm:
        acc = acc * lax.rsqrt(jnp.mean(acc * acc, axis=-1, keepdims=True) + NORM_EPS) * gfin_ref[...]
    o_ref[...] = acc


def _out_ffn(layer, x, mix, w_out, g, w_gu, w_down, g_final, final_norm):
    n = x.shape[0]
    tm = min(n, 512)
    assert n % tm == 0
    rows = lambda i: (i, 0)
    const = lambda i: (0, 0)
    resident = functools.partial(_layer_spec, layer, pipeline_mode=pl.Buffered(1))
    return pl.pallas_call(
        functools.partial(_out_ffn_kernel, final_norm=final_norm),
        grid=(n // tm,),
        in_specs=[pl.BlockSpec((tm, D_MODEL), rows),
                  pl.BlockSpec((tm, MIX_WIDTH), rows),
                  resident((MIX_WIDTH, D_MODEL)),
                  _layer_spec(layer, (1, D_MODEL)),
                  resident((D_MODEL, 2 * FFN_HIDDEN)),
                  resident((FFN_HIDDEN, D_MODEL)),
                  pl.BlockSpec((1, D_MODEL), const)],
        out_specs=pl.BlockSpec((tm, D_MODEL), rows),
        out_shape=jax.ShapeDtypeStruct((n, D_MODEL), F32),
        compiler_params=pltpu.CompilerParams(
            dimension_semantics=("parallel",), vmem_limit_bytes=VMEM_LIMIT),
        name="out_ffn",
    )(x, mix, w_out, g, w_gu, w_down, g_final)


def _gate_row(v, offset):
    depth, k = v.shape
    return jnp.zeros((depth, 1, GATE_LANES), F32).at[:, 0, offset:offset + k].set(v)


def _gate_col(v, offset):
    depth, k = v.shape
    return jnp.zeros((depth, GATE_ROWS, 1), F32).at[:, offset:offset + k, 0].set(v)


def kernel(x, positions, norm_mix, w_in, ml_conv_w, ml_conv_b, ml_gate_bias, ml_norm,
           ssd_conv_w, ssd_conv_b, ssd_dt_bias, ssd_a_log, ssd_d, ssd_norm, ret_norm,
           w_out, norm_ffn, w_gate_up, w_down, norm_final):
    bsz, seq, _ = x.shape
    depth = w_in.shape[0]
    n = bsz * seq
    nc = seq // CHUNK
    assert seq % CHUNK == 0 and x.shape[2] == D_MODEL and w_in.shape[2] == ML_IN + SSD_IN + RET_IN

    ml0, ssd0, ret0 = 0, ML_IN, ML_IN + SSD_IN
    w_in_b = w_in.astype(BF16)
    seg = lambda start, width: w_in_b[:, :, start:start + width]
    ml_q, ml_k = seg(ml0, ML_WIDTH), seg(ml0 + ML_WIDTH, ML_WIDTH)
    ml_v, ml_o = seg(ml0 + 2 * ML_WIDTH, ML_WIDTH), seg(ml0 + 3 * ML_WIDTH, ML_WIDTH)
    ml_if = seg(ml0 + 4 * ML_WIDTH, 2 * ML_HEADS)
    ssd_z, ssd_x = seg(ssd0, SSD_WIDTH), seg(ssd0 + SSD_WIDTH, SSD_WIDTH)
    ssd_bc = seg(ssd0 + 2 * SSD_WIDTH, 2 * SSD_BC_WIDTH)
    ssd_dt = seg(ssd0 + SSD_WIDTH + SSD_CONV_DIM, SSD_HEADS)
    ret_qk, ret_v = seg(ret0, 2 * RET_WIDTH), seg(ret0 + 2 * RET_WIDTH, RET_WIDTH)
    ret_g = seg(ret0 + 3 * RET_WIDTH, RET_WIDTH)
    w_wide = jnp.concatenate([ret_qk, ret_g, ml_q, ml_k, ssd_x, ml_o, ssd_z, ssd_bc], axis=-1)
    w_vml_t = jnp.swapaxes(ml_v, 1, 2)
    w_vret_t = jnp.swapaxes(ret_v, 1, 2)
    w_gate_cols = jnp.concatenate([ml_if, ssd_dt], axis=-1)
    n_gate = w_gate_cols.shape[-1]
    w_nar = jnp.pad(w_gate_cols, ((0, 0), (0, 0), (0, GATE_LANES - n_gate)))
    w_nar_t = jnp.pad(jnp.swapaxes(w_gate_cols, 1, 2), ((0, 0), (0, GATE_ROWS - n_gate), (0, 0)))
    conv_w = jnp.concatenate([ml_conv_w, ssd_conv_w], axis=-1)
    conv_b = jnp.concatenate([ml_conv_b, ssd_conv_b], axis=-1)[:, None, :]
    w_out_b = w_out.astype(BF16)
    w_gu_b = w_gate_up.astype(BF16)
    w_down_b = w_down.astype(BF16)

    gate_bias = jnp.concatenate([ml_gate_bias, ssd_dt_bias], axis=-1)
    bias_row, bias_col = _gate_row(gate_bias, 0), _gate_col(gate_bias, 0)
    alog_row, alog_col = _gate_row(ssd_a_log, DT_LANE), _gate_col(ssd_a_log, DT_LANE)
    row3 = lambda v: v[:, None, :]
    norm_mix3, norm_ffn3 = row3(norm_mix), row3(norm_ffn)
    ml_norm3, ssd_norm3, ret_norm3 = row3(ml_norm), row3(ssd_norm), row3(ret_norm)
    g_final = norm_final.reshape(1, D_MODEL)

    cc, ss = _rope_tables(positions)
    cc = cc.reshape(bsz, seq, RET_HEAD_DIM)
    ss = ss.reshape(bsz, seq, RET_HEAD_DIM)
    xf = x.reshape(n, D_MODEL)
    for l in range(depth):
        *segs, vml_t, vret_t, gates, gates_t = _inproj(
            l, xf, norm_mix3, w_wide, w_vml_t, w_vret_t, w_nar, w_nar_t, conv_w, conv_b, seq)
        segs = [a.reshape(bsz, seq, a.shape[-1]) for a in segs]
        gates = gates.reshape(bsz, seq, GATE_LANES)
        mix = _mixers(l, ssd_d, segs, gates, cc, ss, vml_t, vret_t, gates_t, bias_row, bias_col,
                      alog_row, alog_col, ml_norm3, ssd_norm3, ret_norm3, bsz, nc)
        xf = _out_ffn(l, xf, mix.reshape(n, MIX_WIDTH), w_out_b, norm_ffn3, w_gu_b, w_down_b,
                      g_final, final_norm=(l == depth - 1))
    return xf.reshape(bsz, seq, D_MODEL)
```

```python
import functools
import math

import jax
import jax.numpy as jnp
from jax import lax
from jax.experimental import pallas as pl
from jax.experimental.pallas import tpu as pltpu

F32 = jnp.float32
BF16 = jnp.bfloat16

D_MODEL = 1024
CHUNK = 128
CONV_K = 4
NORM_EPS = 1e-6
ML_HEADS = 6
ML_HEAD_DIM = 128
ML_WIDTH = ML_HEADS * ML_HEAD_DIM
SSD_HEADS = 12
SSD_HEAD_DIM = 64
SSD_WIDTH = SSD_HEADS * SSD_HEAD_DIM
SSD_GROUPS = 2
SSD_HEADS_PER_GROUP = SSD_HEADS // SSD_GROUPS
SSD_GROUP_WIDTH = SSD_HEADS_PER_GROUP * SSD_HEAD_DIM
SSD_STATE = 128
SSD_BC_WIDTH = SSD_GROUPS * SSD_STATE
SSD_CONV_DIM = SSD_WIDTH + 2 * SSD_BC_WIDTH
RET_HEADS = 4
RET_HEAD_DIM = 128
RET_WIDTH = RET_HEADS * RET_HEAD_DIM
ROPE_BASE = 10000.0
MIX_WIDTH = ML_WIDTH + SSD_WIDTH + RET_WIDTH
Y_ML, Y_SSD, Y_RET = 0, ML_WIDTH, ML_WIDTH + SSD_WIDTH
ML_IN = 4 * ML_WIDTH + 2 * ML_HEADS
SSD_IN = SSD_WIDTH + SSD_CONV_DIM + SSD_HEADS
RET_IN = 4 * RET_WIDTH
FFN_HIDDEN = 2816

SEGMENTS = (
    ("ret_q", RET_WIDTH, "plain"),
    ("ret_k", RET_WIDTH, "plain"),
    ("ret_g", RET_WIDTH, "silu"),
    ("ml_q", ML_WIDTH, "conv"),
    ("ml_k", ML_WIDTH, "conv_scaled"),
    ("ssd_x", SSD_WIDTH, "conv"),
    ("ml_o", ML_WIDTH, "sigmoid"),
    ("ssd_z", SSD_WIDTH, "silu"),
    ("ssd_b", SSD_BC_WIDTH, "conv"),
    ("ssd_c", SSD_BC_WIDTH, "conv"),
)
SEG_INDEX = {name: i for i, (name, _, _) in enumerate(SEGMENTS)}
WIDE_WIDTH = sum(w for _, w, _ in SEGMENTS)
PROJ_BLOCK = 256
CONV_WIDTH = 2 * ML_WIDTH + SSD_CONV_DIM

GATE_LANES = 128
GATE_ROWS = 32
I_LANE = 0
F_LANE = ML_HEADS
DT_LANE = 2 * ML_HEADS
ML_GATE_ROWS = 16
DT_ROW0 = 8
TAIL = 8

VMEM_LIMIT = 56 * 1024 * 1024

_NT = (((1,), (1,)), ((), ()))


def _sigmoid(x):
    return 1.0 / (1.0 + jnp.exp2(x * (-1.0 / math.log(2.0))))


def _silu(x):
    return x * _sigmoid(x)


def _log_sigmoid(x):
    return jnp.minimum(x, 0.0) - jnp.log1p(jnp.exp(-jnp.abs(x)))


def _softplus(x):
    return jnp.maximum(x, 0.0) + jnp.log1p(jnp.exp(-jnp.abs(x)))


def _dot(a, b):
    return jnp.dot(a, b, preferred_element_type=F32)


def _dot_nt(a, b):
    return lax.dot_general(a, b, _NT, preferred_element_type=F32)


def _dot_exact(a, b):
    return jnp.dot(a, b, preferred_element_type=F32, precision=lax.Precision.HIGHEST)


def _row_vec_dot(vec, mat_nt=None, mat=None):
    rows = 16
    v = jnp.broadcast_to(vec, (rows, vec.shape[1]))
    hi = v.astype(BF16).astype(F32)
    r = lax.broadcasted_iota(jnp.int32, v.shape, 0)
    lhs = jnp.where(r == 0, hi, jnp.where(r == 1, v - hi, 0.0)).astype(BF16)
    out = _dot_nt(lhs, mat_nt) if mat_nt is not None else _dot(lhs, mat)
    return jnp.sum(out, axis=0, keepdims=True)


def _chunk_iotas():
    row = lax.broadcasted_iota(jnp.int32, (CHUNK, CHUNK), 0)
    col = lax.broadcasted_iota(jnp.int32, (CHUNK, CHUNK), 1)
    return row, col


def _rope_kernel(pos_ref, invf_ref, cc_ref, ss_ref):
    ang = pos_ref[...].astype(F32) * invf_ref[...]
    lane = lax.broadcasted_iota(jnp.int32, ang.shape, 1)
    sin = jnp.sin(ang)
    cc_ref[...] = jnp.cos(ang)
    ss_ref[...] = jnp.where(lane < RET_HEAD_DIM // 2, -sin, sin)


def _rope_tables(positions):
    n = positions.size
    tm = min(n, 2048)
    assert n % tm == 0
    half = RET_HEAD_DIM // 2
    inv_freq = ROPE_BASE ** (-jnp.arange(0, RET_HEAD_DIM, 2, dtype=F32) / RET_HEAD_DIM)
    invf = jnp.concatenate([inv_freq, inv_freq]).reshape(1, 2 * half)
    return pl.pallas_call(
        _rope_kernel,
        grid=(n // tm,),
        in_specs=[pl.BlockSpec((tm, 1), lambda i: (i, 0)),
                  pl.BlockSpec((1, RET_HEAD_DIM), lambda i: (0, 0))],
        out_specs=[pl.BlockSpec((tm, RET_HEAD_DIM), lambda i: (i, 0))] * 2,
        out_shape=[jax.ShapeDtypeStruct((n, RET_HEAD_DIM), F32)] * 2,
        name="rope_tables",
    )(positions.reshape(n, 1), invf)


def _inproj_kernel(*refs, tiles_per_seq):
    x_ref, g_ref, w_ref, wvml_ref, wvret_ref, wnt_ref, cw_ref, cb_ref = refs[0:8]
    seg_refs = refs[8:8 + len(SEGMENTS)]
    vml_ref, vret_ref, gates_ref, gatest_ref, h_ref, cbuf = refs[8 + len(SEGMENTS):]
    tm = x_ref.shape[0]

    @pl.when(pl.program_id(0) % tiles_per_seq == 0)
    def _():
        cbuf[0:TAIL, :] = jnp.zeros((TAIL, cbuf.shape[1]), F32)

    def store_chunks(ref, rows, value):
        for c in range(tm // CHUNK):
            ref[c, rows, :] = value[:, c * CHUNK:(c + 1) * CHUNK]

    x = x_ref[...]
    y = x * lax.rsqrt(jnp.mean(x * x, axis=-1, keepdims=True) + NORM_EPS) * g_ref[...]
    h_ref[...] = y.astype(BF16)
    gates_t = _dot_nt(wnt_ref[...], h_ref[...])
    store_chunks(gatest_ref, slice(0, GATE_ROWS), gates_t)
    padded = jnp.concatenate([gates_t, jnp.zeros((GATE_LANES - GATE_ROWS, tm), F32)], axis=0)
    for c in range(tm // CHUNK):
        gates_ref[c * CHUNK:(c + 1) * CHUNK, :] = padded[:, c * CHUNK:(c + 1) * CHUNK].T

    def conv_epilogue(acc, o_ref, cols, cc, scaled):
        cbuf[TAIL:TAIL + tm, cc] = acc
        out = cb_ref[:, cc] + cw_ref[CONV_K - 1:CONV_K, cc] * acc
        for j in range(1, CONV_K):
            out = out + cw_ref[CONV_K - 1 - j:CONV_K - j, cc] * cbuf[TAIL - j:TAIL - j + tm, cc]
        cbuf[0:TAIL, cc] = cbuf[tm:tm + TAIL, cc]
        out = _silu(out)
        if scaled:
            out = out * (ML_HEAD_DIM ** -0.5)
        o_ref[:, cols] = out.astype(BF16)

    def plain_epilogue(acc, o_ref, cols, act):
        o_ref[:, cols] = (acc if act is None else act(acc)).astype(BF16)

    def t_epilogue(acc, ref, rows):
        store_chunks(ref, rows, acc.astype(BF16))

    heavy, light = [], []
    conv_col = 0
    start = 0
    for o_ref, (_, width, kind) in zip(seg_refs, SEGMENTS):
        for blk in range(0, width, PROJ_BLOCK):
            cols = slice(blk, blk + PROJ_BLOCK)
            wcols = slice(start + blk, start + blk + PROJ_BLOCK)
            mm = functools.partial(lambda c: _dot(h_ref[...], w_ref[:, c]), wcols)
            if kind in ("conv", "conv_scaled"):
                cc = slice(conv_col, conv_col + PROJ_BLOCK)
                conv_col += PROJ_BLOCK
                heavy.append((mm, functools.partial(conv_epilogue, o_ref=o_ref, cols=cols, cc=cc,
                                                    scaled=(kind == "conv_scaled"))))
            else:
                act = {"silu": _silu, "sigmoid": _sigmoid, "plain": None}[kind]
                light.append((mm, functools.partial(plain_epilogue, o_ref=o_ref, cols=cols, act=act)))
        start += width
    for w_t_ref, out_ref, width in ((wvml_ref, vml_ref, ML_WIDTH), (wvret_ref, vret_ref, RET_WIDTH)):
        for blk in range(0, width, PROJ_BLOCK):
            rows = slice(blk, blk + PROJ_BLOCK)
            mm = functools.partial(lambda r, w: _dot_nt(w[r, :], h_ref[...]), rows, w_t_ref)
            light.append((mm, functools.partial(t_epilogue, ref=out_ref, rows=rows)))

    jobs = []
    while heavy or light:
        if heavy:
            jobs.append(heavy.pop(0))
        if light:
            jobs.append(light.pop(0))
        if light and len(light) > len(heavy):
            jobs.append(light.pop(0))
    acc = jobs[0][0]()
    for i, (_, epilogue) in enumerate(jobs):
        nxt = jobs[i + 1][0]() if i + 1 < len(jobs) else None
        epilogue(acc)
        acc = nxt


def _layer_spec(layer, shape, **kwargs):
    return pl.BlockSpec((None,) + tuple(shape), lambda *_: (layer,) + (0,) * len(shape), **kwargs)


def _inproj(layer, x, g, w_wide, w_vml_t, w_vret_t, w_nar_t, conv_w, conv_b, seq):
    n = x.shape[0]
    tm = min(seq, 512)
    assert seq % tm == 0 and n % seq == 0 and tm % CHUNK == 0
    rows = lambda i: (i, 0)
    chunks = lambda i: (i, 0, 0)
    cpt = tm // CHUNK
    resident = functools.partial(_layer_spec, layer, pipeline_mode=pl.Buffered(1))
    return pl.pallas_call(
        functools.partial(_inproj_kernel, tiles_per_seq=seq // tm),
        grid=(n // tm,),
        in_specs=[pl.BlockSpec((tm, D_MODEL), rows),
                  _layer_spec(layer, (1, D_MODEL)),
                  resident((D_MODEL, WIDE_WIDTH)),
                  resident((ML_WIDTH, D_MODEL)),
                  resident((RET_WIDTH, D_MODEL)),
                  resident((GATE_ROWS, D_MODEL)),
                  _layer_spec(layer, (CONV_K, CONV_WIDTH)),
                  _layer_spec(layer, (1, CONV_WIDTH))],
        out_specs=[pl.BlockSpec((tm, width), rows) for _, width, _ in SEGMENTS]
                  + [pl.BlockSpec((cpt, ML_WIDTH, CHUNK), chunks),
                     pl.BlockSpec((cpt, RET_WIDTH, CHUNK), chunks),
                     pl.BlockSpec((tm, GATE_LANES), rows),
                     pl.BlockSpec((cpt, GATE_ROWS, CHUNK), chunks)],
        out_shape=[jax.ShapeDtypeStruct((n, width), BF16) for _, width, _ in SEGMENTS]
                  + [jax.ShapeDtypeStruct((n // CHUNK, ML_WIDTH, CHUNK), BF16),
                     jax.ShapeDtypeStruct((n // CHUNK, RET_WIDTH, CHUNK), BF16),
                     jax.ShapeDtypeStruct((n, GATE_LANES), F32),
                     jax.ShapeDtypeStruct((n // CHUNK, GATE_ROWS, CHUNK), F32)],
        scratch_shapes=[pltpu.VMEM((tm, D_MODEL), BF16),
                        pltpu.VMEM((TAIL + tm, CONV_WIDTH), F32)],
        compiler_params=pltpu.CompilerParams(
            dimension_semantics=("arbitrary",), vmem_limit_bytes=VMEM_LIMIT),
        name="inproj",
    )(x, g, w_wide, w_vml_t, w_vret_t, w_nar_t, conv_w, conv_b)


def _mlstm_body(q_ref, k_ref, o_ref, gates_ref, vt_refs, gatest_refs, brow_ref, bcol_ref, norm_ref,
                y_ref, ct_ref, n_ref, m_ref):
    bsz = len(vt_refs)

    @pl.when(pl.program_id(0) == 0)
    def _():
        ct_ref[...] = jnp.zeros(ct_ref.shape, F32)
        n_ref[...] = jnp.zeros(n_ref.shape, F32)
        m_ref[...] = jnp.zeros(m_ref.shape, F32)

    row, col = _chunk_iotas()
    causal_t = row <= col
    tri = (col <= row).astype(F32)
    tri_t = causal_t.astype(F32)

    gate_terms = []
    for b in range(bsz):
        g_col = gates_ref[b] + brow_ref[...]
        g_row = gatest_refs[b][0:ML_GATE_ROWS, :] + bcol_ref[0:ML_GATE_ROWS, :]
        b_col = _dot_exact(tri, _log_sigmoid(g_col))
        b_row = _dot_exact(_log_sigmoid(g_row), tri_t)
        gate_terms.append((g_row, b_row, g_col - pltpu.roll(b_col, GATE_LANES - F_LANE, 1)))

    units = [(b, h) for b in range(bsz) for h in range(ML_HEADS)]
    head = lambda h: slice(h * ML_HEAD_DIM, (h + 1) * ML_HEAD_DIM)

    stage1 = []
    for st, (b, h) in enumerate(units):
        g_row, b_row, cs_all = gate_terms[b]
        q = q_ref[b, :, head(h)]
        k = k_ref[b, :, head(h)]
        m_prev = m_ref[st][:, 0:1]
        b_t = b_row[F_LANE + h:F_LANE + h + 1, :]
        cs = cs_all[:, h:h + 1]
        log_d = jnp.where(causal_t, b_t + cs, -jnp.inf)
        a = b_t + m_prev
        m_t = jnp.maximum(a, jnp.max(log_d, axis=0, keepdims=True))
        inter = jnp.exp(a - m_t)
        scores = _dot_nt(k, q) * jnp.exp(log_d - m_t)
        den = jnp.sum(scores, axis=0, keepdims=True) + inter * _row_vec_dot(n_ref[st], mat_nt=q)
        scale = 1.0 / jnp.maximum(jnp.abs(den), jnp.exp(-m_t))
        cross = _dot_nt(ct_ref[st].astype(BF16), q) * (inter * scale)
        stage1.append((scores.astype(BF16), cross, scale))

    outs = []
    for st, (b, h) in enumerate(units):
        scores, cross, scale = stage1[st]
        out = _dot(vt_refs[b][head(h), :], scores) * scale + cross
        outs.append(out * lax.rsqrt(jnp.mean(out * out, axis=0, keepdims=True) + NORM_EPS))

    for st, (b, h) in enumerate(units):
        g_row, b_row, _ = gate_terms[b]
        k = k_ref[b, :, head(h)]
        vt = vt_refs[b][head(h), :]
        m_prev = m_ref[st][:, 0:1]
        b_t = b_row[F_LANE + h:F_LANE + h + 1, :]
        i_t = g_row[I_LANE + h:I_LANE + h + 1, :]
        b_last = b_t[:, CHUNK - 1:CHUNK]
        w_state = b_last - b_t + i_t
        m_new = jnp.maximum(b_last + m_prev, jnp.max(w_state, axis=1, keepdims=True))
        a_old = jnp.exp(b_last + m_prev - m_new)
        e = jnp.exp(w_state - m_new)
        ct_ref[st] = a_old * ct_ref[st] + _dot((vt.astype(F32) * e).astype(BF16), k)
        n_ref[st] = a_old * n_ref[st] + _row_vec_dot(e, mat=k)
        m_ref[st] = jnp.broadcast_to(m_new, m_ref.shape[1:])

    for st, (b, h) in enumerate(units):
        y_ref[b, :, Y_ML + h * ML_HEAD_DIM:Y_ML + (h + 1) * ML_HEAD_DIM] = (
            outs[st].T * norm_ref[:, head(h)] * o_ref[b, :, head(h)].astype(F32)).astype(BF16)


def _per_batch_specs(rows, bsz, nc):
    return [pl.BlockSpec((None, rows, CHUNK), functools.partial(lambda b, c: (b * nc + c, 0, 0), b))
            for b in range(bsz)]


def _ssd_body(d_ref, x_ref, b_ref, c_ref, z_ref, gates_ref, gatest_refs, brow_ref, bcol_ref,
              alog_row_ref, alog_col_ref, norm_ref, y_ref, st_ref, xt_buf, yt_buf, layer):
    bsz = len(gatest_refs)

    @pl.when(pl.program_id(0) == 0)
    def _():
        st_ref[...] = jnp.zeros(st_ref.shape, F32)

    row, col = _chunk_iotas()
    causal_t = row <= col
    tri = (col <= row).astype(F32)
    tri_t = causal_t.astype(F32)
    rows16 = slice(DT_ROW0, DT_ROW0 + 16)
    a_neg_row = -jnp.exp(alog_row_ref[...])
    a_neg_col = -jnp.exp(alog_col_ref[rows16, :])

    gate_terms = []
    group_terms = []
    for b in range(bsz):
        dt_col = _softplus(gates_ref[b] + brow_ref[...])
        dt_row = _softplus(gatest_refs[b][rows16, :] + bcol_ref[rows16, :])
        acum_col = _dot_exact(tri, dt_col * a_neg_row)
        acum_row = _dot_exact(dt_row * a_neg_col, tri_t)
        gate_terms.append((dt_row, acum_col, acum_row))
        for blk in range(0, SSD_WIDTH, 128):
            xt_buf[b, blk:blk + 128, :] = x_ref[b, :, blk:blk + 128].astype(F32).T
        for g in range(SSD_GROUPS):
            gs = slice(g * SSD_STATE, (g + 1) * SSD_STATE)
            b_g = b_ref[b, :, gs]
            c_g = c_ref[b, :, gs]
            prev = st_ref[b * SSD_GROUPS + g]
            group_terms.append((b_g, prev, _dot_nt(b_g, c_g),
                                _dot_nt(prev.astype(BF16), c_g)))

    units = [(b, h) for b in range(bsz) for h in range(SSD_HEADS)]
    head_terms = []
    for b, h in units:
        dt_row, acum_col, acum_row = gate_terms[b]
        cb = group_terms[b * SSD_GROUPS + h // SSD_HEADS_PER_GROUP][2]
        r = DT_LANE + h - DT_ROW0
        lane = DT_LANE + h
        a_t = acum_row[r:r + 1, :]
        a_s = acum_col[:, lane:lane + 1]
        l_dec = jnp.exp(jnp.where(causal_t, a_t - a_s, -jnp.inf))
        x_h = xt_buf[b, h * SSD_HEAD_DIM:(h + 1) * SSD_HEAD_DIM, :]
        xc = x_h * dt_row[r:r + 1, :]
        a_last = a_t[:, CHUNK - 1:CHUNK]
        head_terms.append(((cb * l_dec).astype(BF16), x_h, xc.astype(BF16),
                           (xc * jnp.exp(a_last - a_t)).astype(BF16), jnp.exp(a_t), jnp.exp(a_last)))

    for i, (b, h) in enumerate(units):
        m_h, x_h, xc_b, _, exp_a, _ = head_terms[i]
        g, hh = divmod(h, SSD_HEADS_PER_GROUP)
        y_off = group_terms[b * SSD_GROUPS + g][3]
        hs = slice(h * SSD_HEAD_DIM, (h + 1) * SSD_HEAD_DIM)
        yt_buf[b, hs, :] = (_dot(xc_b, m_h)
                            + y_off[hh * SSD_HEAD_DIM:(hh + 1) * SSD_HEAD_DIM, :] * exp_a
                            + d_ref[layer, h] * x_h)

    for b in range(bsz):
        for g in range(SSD_GROUPS):
            st = b * SSD_GROUPS + g
            b_g, prev = group_terms[st][0:2]
            terms = head_terms[b * SSD_HEADS + g * SSD_HEADS_PER_GROUP:
                               b * SSD_HEADS + (g + 1) * SSD_HEADS_PER_GROUP]
            upd = _dot(jnp.concatenate([t[3] for t in terms], axis=0), b_g)
            for hh in range(SSD_HEADS_PER_GROUP):
                ps = slice(hh * SSD_HEAD_DIM, (hh + 1) * SSD_HEAD_DIM)
                st_ref[st, ps, :] = terms[hh][5] * prev[ps, :] + upd[ps, :]

    for b in range(bsz):
        for g in range(SSD_GROUPS):
            parts = []
            for blk in range(g * SSD_GROUP_WIDTH, (g + 1) * SSD_GROUP_WIDTH, 128):
                parts.append(yt_buf[b, blk:blk + 128, :].T * z_ref[b, :, blk:blk + 128].astype(F32))
            ssq = sum(jnp.sum(p * p, axis=1, keepdims=True) for p in parts)
            inv = lax.rsqrt(ssq * (1.0 / SSD_GROUP_WIDTH) + NORM_EPS)
            for i, p in enumerate(parts):
                cs = slice(g * SSD_GROUP_WIDTH + i * 128, g * SSD_GROUP_WIDTH + (i + 1) * 128)
                y_ref[b, :, Y_SSD + cs.start:Y_SSD + cs.stop] = (p * inv * norm_ref[:, cs]).astype(BF16)


def _ret_body(q_ref, k_ref, g_ref, cc_ref, ss_ref, vt_refs, norm_ref, y_ref, rt_ref, dmat_ref):
    bsz = len(vt_refs)
    log_gammas = [math.log1p(-2.0 ** (-5.0 - h)) for h in range(RET_HEADS)]

    @pl.when(pl.program_id(0) == 0)
    def _():
        rt_ref[...] = jnp.zeros(rt_ref.shape, F32)
        row, col = _chunk_iotas()
        rel = (col - row).astype(F32)
        for h in range(RET_HEADS):
            dmat_ref[h] = jnp.exp(jnp.where(row <= col, log_gammas[h] * rel, -jnp.inf))

    t_row = lax.broadcasted_iota(jnp.int32, (1, CHUNK), 1).astype(F32)
    half = RET_HEAD_DIM // 2
    scale = RET_HEAD_DIM ** -0.5

    roped = []
    for b in range(bsz):
        cc = cc_ref[b]
        ss = ss_ref[b]
        for h in range(RET_HEADS):
            sl = slice(h * RET_HEAD_DIM, (h + 1) * RET_HEAD_DIM)
            q = q_ref[b, :, sl].astype(F32)
            k = k_ref[b, :, sl].astype(F32)
            roped.append(((q * cc + pltpu.roll(q, half, 1) * ss).astype(BF16),
                          ((k * cc + pltpu.roll(k, half, 1) * ss) * scale).astype(BF16)))

    units = [(b, h) for b in range(bsz) for h in range(RET_HEADS)]
    xis = [jnp.exp(lg * (t_row + 1.0)) for lg in log_gammas]
    zetas = [jnp.exp(lg * (CHUNK - 1.0 - t_row)) for lg in log_gammas]
    scores, cross = [], []
    for st, (b, h) in enumerate(units):
        q, k = roped[st]
        scores.append((_dot_nt(k, q) * dmat_ref[h]).astype(BF16))
        cross.append(_dot_nt(rt_ref[st].astype(BF16), q) * xis[h])
    ys = []
    for st, (b, h) in enumerate(units):
        sl = slice(h * RET_HEAD_DIM, (h + 1) * RET_HEAD_DIM)
        _, k = roped[st]
        vt = vt_refs[b][sl, :]
        ys.append(_dot(vt, scores[st]) + cross[st])
        rt_ref[st] = (math.exp(log_gammas[h] * CHUNK) * rt_ref[st]
                      + _dot((vt.astype(F32) * zetas[h]).astype(BF16), k))
    for st, (b, h) in enumerate(units):
        sl = slice(h * RET_HEAD_DIM, (h + 1) * RET_HEAD_DIM)
        y = ys[st]
        yc = y - jnp.mean(y, axis=0, keepdims=True)
        yn = yc * lax.rsqrt(jnp.mean(yc * yc, axis=0, keepdims=True) + NORM_EPS)
        y_ref[b, :, Y_RET + sl.start:Y_RET + sl.stop] = (
            yn.T * norm_ref[:, sl] * g_ref[b, :, sl].astype(F32)).astype(BF16)


def _mixers_kernel(*refs, bsz, layer):
    it = iter(refs)
    take = lambda k=None: next(it) if k is None else [next(it) for _ in range(k)]
    d_ref = take()
    ml_q, ml_k, ml_o, ssd_x, ssd_b, ssd_c, ssd_z, ret_q, ret_k, ret_g, gates_ref, cc_ref, ss_ref = take(13)
    vml_refs, vret_refs, gatest_refs = take(bsz), take(bsz), take(bsz)
    brow_ref, bcol_ref, alog_row_ref, alog_col_ref, ml_norm_ref, ssd_norm_ref, ret_norm_ref = take(7)
    y_ref = take()
    ct_ref, n_ref, m_ref, st_ref, xt_buf, yt_buf, rt_ref, dmat_ref = take(8)
    _mlstm_body(ml_q, ml_k, ml_o, gates_ref, vml_refs, gatest_refs, brow_ref, bcol_ref, ml_norm_ref,
                y_ref, ct_ref, n_ref, m_ref)
    _ssd_body(d_ref, ssd_x, ssd_b, ssd_c, ssd_z, gates_ref, gatest_refs, brow_ref, bcol_ref,
              alog_row_ref, alog_col_ref, ssd_norm_ref, y_ref, st_ref, xt_buf, yt_buf, layer)
    _ret_body(ret_q, ret_k, ret_g, cc_ref, ss_ref, vret_refs, ret_norm_ref, y_ref, rt_ref, dmat_ref)


def _mixers(layer, d, segs, gates, cc, ss, vml_t, vret_t, gates_t, bias_row, bias_col, alog_row,
            alog_col, ml_norm, ssd_norm, ret_norm, bsz, nc):
    tok3 = lambda c: (0, c, 0)
    names = ("ml_q", "ml_k", "ml_o", "ssd_x", "ssd_b", "ssd_c", "ssd_z", "ret_q", "ret_k", "ret_g")
    seg_args = [segs[SEG_INDEX[name]] for name in names]
    return pl.pallas_call(
        functools.partial(_mixers_kernel, bsz=bsz, layer=layer),
        grid=(nc,),
        in_specs=[pl.BlockSpec(memory_space=pltpu.SMEM)]
                 + [pl.BlockSpec((bsz, CHUNK, a.shape[-1]), tok3) for a in seg_args + [gates, cc, ss]]
                 + _per_batch_specs(ML_WIDTH, bsz, nc) + _per_batch_specs(RET_WIDTH, bsz, nc)
                 + _per_batch_specs(GATE_ROWS, bsz, nc)
                 + [_layer_spec(layer, (1, GATE_LANES)),
                    _layer_spec(layer, (GATE_ROWS, 1)),
                    _layer_spec(layer, (1, GATE_LANES)),
                    _layer_spec(layer, (GATE_ROWS, 1)),
                    _layer_spec(layer, (1, ML_WIDTH)),
                    _layer_spec(layer, (1, SSD_WIDTH)),
                    _layer_spec(layer, (1, RET_WIDTH))],
        out_specs=pl.BlockSpec((bsz, CHUNK, MIX_WIDTH), tok3),
        out_shape=jax.ShapeDtypeStruct((bsz, nc * CHUNK, MIX_WIDTH), BF16),
        scratch_shapes=[pltpu.VMEM((bsz * ML_HEADS, ML_HEAD_DIM, ML_HEAD_DIM), F32),
                        pltpu.VMEM((bsz * ML_HEADS, 1, ML_HEAD_DIM), F32),
                        pltpu.VMEM((bsz * ML_HEADS, 1, 128), F32),
                        pltpu.VMEM((bsz * SSD_GROUPS, SSD_GROUP_WIDTH, SSD_STATE), F32),
                        pltpu.VMEM((bsz, SSD_WIDTH, CHUNK), F32),
                        pltpu.VMEM((bsz, SSD_WIDTH, CHUNK), F32),
                        pltpu.VMEM((bsz * RET_HEADS, RET_HEAD_DIM, RET_HEAD_DIM), F32),
                        pltpu.VMEM((RET_HEADS, CHUNK, CHUNK), F32)],
        compiler_params=pltpu.CompilerParams(
            dimension_semantics=("arbitrary",), vmem_limit_bytes=VMEM_LIMIT),
        name="mixers",
    )(d, *seg_args, gates, cc, ss, *([vml_t] * bsz), *([vret_t] * bsz), *([gates_t] * bsz),
      bias_row, bias_col, alog_row, alog_col, ml_norm, ssd_norm, ret_norm)


FFN_BLOCK = 256


def _out_ffn_kernel(x_ref, mix_ref, wo_ref, g_ref, wgu_ref, wd_ref, gfin_ref, o_ref, *, final_norm):
    x1 = x_ref[...] + _dot(mix_ref[...], wo_ref[...])
    h = (x1 * lax.rsqrt(jnp.mean(x1 * x1, axis=-1, keepdims=True) + NORM_EPS) * g_ref[...]).astype(BF16)
    acc = x1
    for blk in range(0, FFN_HIDDEN, FFN_BLOCK):
        gate = _dot(h, wgu_ref[:, blk:blk + FFN_BLOCK])
        up = _dot(h, wgu_ref[:, FFN_HIDDEN + blk:FFN_HIDDEN + blk + FFN_BLOCK])
        acc = acc + _dot((_silu(gate) * up).astype(BF16), wd_ref[blk:blk + FFN_BLOCK, :])
    if final_norm:
        acc = acc * lax.rsqrt(jnp.mean(acc * acc, axis=-1, keepdims=True) + NORM_EPS) * gfin_ref[...]
    o_ref[...] = acc


def _out_ffn(layer, x, mix, w_out, g, w_gu, w_down, g_final, final_norm):
    n = x.shape[0]
    tm = min(n, 1024)
    assert n % tm == 0
    rows = lambda i: (i, 0)
    const = lambda i: (0, 0)
    resident = functools.partial(_layer_spec, layer, pipeline_mode=pl.Buffered(1))
    return pl.pallas_call(
        functools.partial(_out_ffn_kernel, final_norm=final_norm),
        grid=(n // tm,),
        in_specs=[pl.BlockSpec((tm, D_MODEL), rows),
                  pl.BlockSpec((tm, MIX_WIDTH), rows),
                  resident((MIX_WIDTH, D_MODEL)),
                  _layer_spec(layer, (1, D_MODEL)),
                  resident((D_MODEL, 2 * FFN_HIDDEN)),
                  resident((FFN_HIDDEN, D_MODEL)),
                  pl.BlockSpec((1, D_MODEL), const)],
        out_specs=pl.BlockSpec((tm, D_MODEL), rows),
        out_shape=jax.ShapeDtypeStruct((n, D_MODEL), F32),
        compiler_params=pltpu.CompilerParams(
            dimension_semantics=("parallel",), vmem_limit_bytes=VMEM_LIMIT),
        name="out_ffn",
    )(x, mix, w_out, g, w_gu, w_down, g_final)


def _gate_row(v, offset):
    depth, k = v.shape
    return jnp.zeros((depth, 1, GATE_LANES), F32).at[:, 0, offset:offset + k].set(v)


def _gate_col(v, offset):
    depth, k = v.shape
    return jnp.zeros((depth, GATE_ROWS, 1), F32).at[:, offset:offset + k, 0].set(v)


def kernel(x, positions, norm_mix, w_in, ml_conv_w, ml_conv_b, ml_gate_bias, ml_norm,
           ssd_conv_w, ssd_conv_b, ssd_dt_bias, ssd_a_log, ssd_d, ssd_norm, ret_norm,
           w_out, norm_ffn, w_gate_up, w_down, norm_final):
    bsz, seq, _ = x.shape
    depth = w_in.shape[0]
    n = bsz * seq
    nc = seq // CHUNK
    assert seq % CHUNK == 0 and x.shape[2] == D_MODEL and w_in.shape[2] == ML_IN + SSD_IN + RET_IN

    ml0, ssd0, ret0 = 0, ML_IN, ML_IN + SSD_IN
    w_in_b = w_in.astype(BF16)
    seg = lambda start, width: w_in_b[:, :, start:start + width]
    ml_q, ml_k = seg(ml0, ML_WIDTH), seg(ml0 + ML_WIDTH, ML_WIDTH)
    ml_v, ml_o = seg(ml0 + 2 * ML_WIDTH, ML_WIDTH), seg(ml0 + 3 * ML_WIDTH, ML_WIDTH)
    ml_if = seg(ml0 + 4 * ML_WIDTH, 2 * ML_HEADS)
    ssd_z, ssd_x = seg(ssd0, SSD_WIDTH), seg(ssd0 + SSD_WIDTH, SSD_WIDTH)
    ssd_bc = seg(ssd0 + 2 * SSD_WIDTH, 2 * SSD_BC_WIDTH)
    ssd_dt = seg(ssd0 + SSD_WIDTH + SSD_CONV_DIM, SSD_HEADS)
    ret_qk, ret_v = seg(ret0, 2 * RET_WIDTH), seg(ret0 + 2 * RET_WIDTH, RET_WIDTH)
    ret_g = seg(ret0 + 3 * RET_WIDTH, RET_WIDTH)
    w_wide = jnp.concatenate([ret_qk, ret_g, ml_q, ml_k, ssd_x, ml_o, ssd_z, ssd_bc], axis=-1)
    w_vml_t = jnp.swapaxes(ml_v, 1, 2)
    w_vret_t = jnp.swapaxes(ret_v, 1, 2)
    w_gate_cols = jnp.concatenate([ml_if, ssd_dt], axis=-1)
    n_gate = w_gate_cols.shape[-1]
    w_nar_t = jnp.pad(jnp.swapaxes(w_gate_cols, 1, 2), ((0, 0), (0, GATE_ROWS - n_gate), (0, 0)))
    conv_w = jnp.concatenate([ml_conv_w, ssd_conv_w], axis=-1)
    conv_b = jnp.concatenate([ml_conv_b, ssd_conv_b], axis=-1)[:, None, :]
    w_out_b = w_out.astype(BF16)
    w_gu_b = w_gate_up.astype(BF16)
    w_down_b = w_down.astype(BF16)

    gate_bias = jnp.concatenate([ml_gate_bias, ssd_dt_bias], axis=-1)
    bias_row, bias_col = _gate_row(gate_bias, 0), _gate_col(gate_bias, 0)
    alog_row, alog_col = _gate_row(ssd_a_log, DT_LANE), _gate_col(ssd_a_log, DT_LANE)
    row3 = lambda v: v[:, None, :]
    norm_mix3, norm_ffn3 = row3(norm_mix), row3(norm_ffn)
    ml_norm3, ssd_norm3, ret_norm3 = row3(ml_norm), row3(ssd_norm), row3(ret_norm)
    g_final = norm_final.reshape(1, D_MODEL)

    cc, ss = _rope_tables(positions)
    cc = cc.reshape(bsz, seq, RET_HEAD_DIM)
    ss = ss.reshape(bsz, seq, RET_HEAD_DIM)
    xf = x.reshape(n, D_MODEL)
    for l in range(depth):
        *segs, vml_t, vret_t, gates, gates_t = _inproj(
            l, xf, norm_mix3, w_wide, w_vml_t, w_vret_t, w_nar_t, conv_w, conv_b, seq)
        segs = [a.reshape(bsz, seq, a.shape[-1]) for a in segs]
        gates = gates.reshape(bsz, seq, GATE_LANES)
        mix = _mixers(l, ssd_d, segs, gates, cc, ss, vml_t, vret_t, gates_t, bias_row, bias_col,
                      alog_row, alog_col, ml_norm3, ssd_norm3, ret_norm3, bsz, nc)
        xf = _out_ffn(l, xf, mix.reshape(n, MIX_WIDTH), w_out_b, norm_ffn3, w_gu_b, w_down_b,
                      g_final, final_norm=(l == depth - 1))
    return xf.reshape(bsz, seq, D_MODEL)
```

```python
import functools
import math

import jax
import jax.numpy as jnp
from jax import lax
from jax.experimental import pallas as pl
from jax.experimental.pallas import tpu as pltpu

F32 = jnp.float32
BF16 = jnp.bfloat16

D_MODEL = 1024
CHUNK = 128
CONV_K = 4
NORM_EPS = 1e-6
ML_HEADS = 6
ML_HEAD_DIM = 128
ML_WIDTH = ML_HEADS * ML_HEAD_DIM
SSD_HEADS = 12
SSD_HEAD_DIM = 64
SSD_WIDTH = SSD_HEADS * SSD_HEAD_DIM
SSD_GROUPS = 2
SSD_HEADS_PER_GROUP = SSD_HEADS // SSD_GROUPS
SSD_GROUP_WIDTH = SSD_HEADS_PER_GROUP * SSD_HEAD_DIM
SSD_STATE = 128
SSD_BC_WIDTH = SSD_GROUPS * SSD_STATE
SSD_CONV_DIM = SSD_WIDTH + 2 * SSD_BC_WIDTH
RET_HEADS = 4
RET_HEAD_DIM = 128
RET_WIDTH = RET_HEADS * RET_HEAD_DIM
ROPE_BASE = 10000.0
MIX_WIDTH = ML_WIDTH + SSD_WIDTH + RET_WIDTH
Y_ML, Y_SSD, Y_RET = 0, ML_WIDTH, ML_WIDTH + SSD_WIDTH
ML_IN = 4 * ML_WIDTH + 2 * ML_HEADS
SSD_IN = SSD_WIDTH + SSD_CONV_DIM + SSD_HEADS
RET_IN = 4 * RET_WIDTH
FFN_HIDDEN = 2816

SEGMENTS = (
    ("ret_q", RET_WIDTH, "plain"),
    ("ret_k", RET_WIDTH, "plain"),
    ("ret_g", RET_WIDTH, "silu"),
    ("ml_q", ML_WIDTH, "conv"),
    ("ml_k", ML_WIDTH, "conv_scaled"),
    ("ssd_x", SSD_WIDTH, "conv"),
    ("ml_o", ML_WIDTH, "sigmoid"),
    ("ssd_z", SSD_WIDTH, "silu"),
    ("ssd_b", SSD_BC_WIDTH, "conv"),
    ("ssd_c", SSD_BC_WIDTH, "conv"),
)
SEG_INDEX = {name: i for i, (name, _, _) in enumerate(SEGMENTS)}
WIDE_WIDTH = sum(w for _, w, _ in SEGMENTS)
PROJ_BLOCK = 256
CONV_WIDTH = 2 * ML_WIDTH + SSD_CONV_DIM

GATE_LANES = 128
GATE_ROWS = 32
I_LANE = 0
F_LANE = ML_HEADS
DT_LANE = 2 * ML_HEADS
ML_GATE_ROWS = 16
DT_ROW0 = 8
TAIL = 8

VMEM_LIMIT = 56 * 1024 * 1024

_NT = (((1,), (1,)), ((), ()))


def _sigmoid(x):
    return 1.0 / (1.0 + jnp.exp2(x * (-1.0 / math.log(2.0))))


def _silu(x):
    return x * _sigmoid(x)


def _log_sigmoid(x):
    return jnp.minimum(x, 0.0) - jnp.log1p(jnp.exp(-jnp.abs(x)))


def _softplus(x):
    return jnp.maximum(x, 0.0) + jnp.log1p(jnp.exp(-jnp.abs(x)))


def _dot(a, b):
    return jnp.dot(a, b, preferred_element_type=F32)


def _dot_nt(a, b):
    return lax.dot_general(a, b, _NT, preferred_element_type=F32)


def _dot_exact(a, b):
    return jnp.dot(a, b, preferred_element_type=F32, precision=lax.Precision.HIGHEST)


def _split_rows(vec):
    rows = 16
    v = jnp.broadcast_to(vec, (rows, vec.shape[1]))
    hi = v.astype(BF16).astype(F32)
    r = lax.broadcasted_iota(jnp.int32, v.shape, 0)
    return jnp.where(r == 0, hi, jnp.where(r == 1, v - hi, 0.0)).astype(BF16)


def _chunk_iotas():
    row = lax.broadcasted_iota(jnp.int32, (CHUNK, CHUNK), 0)
    col = lax.broadcasted_iota(jnp.int32, (CHUNK, CHUNK), 1)
    return row, col


def _rope_kernel(pos_ref, invf_ref, cc_ref, ss_ref):
    ang = pos_ref[...].astype(F32) * invf_ref[...]
    lane = lax.broadcasted_iota(jnp.int32, ang.shape, 1)
    sin = jnp.sin(ang)
    cc_ref[...] = jnp.cos(ang)
    ss_ref[...] = jnp.where(lane < RET_HEAD_DIM // 2, -sin, sin)


def _rope_tables(positions):
    n = positions.size
    tm = min(n, 2048)
    assert n % tm == 0
    half = RET_HEAD_DIM // 2
    inv_freq = ROPE_BASE ** (-jnp.arange(0, RET_HEAD_DIM, 2, dtype=F32) / RET_HEAD_DIM)
    invf = jnp.concatenate([inv_freq, inv_freq]).reshape(1, 2 * half)
    return pl.pallas_call(
        _rope_kernel,
        grid=(n // tm,),
        in_specs=[pl.BlockSpec((tm, 1), lambda i: (i, 0)),
                  pl.BlockSpec((1, RET_HEAD_DIM), lambda i: (0, 0))],
        out_specs=[pl.BlockSpec((tm, RET_HEAD_DIM), lambda i: (i, 0))] * 2,
        out_shape=[jax.ShapeDtypeStruct((n, RET_HEAD_DIM), F32)] * 2,
        name="rope_tables",
    )(positions.reshape(n, 1), invf)


def _inproj_kernel(*refs, tiles_per_seq):
    x_ref, g_ref, w_ref, wvml_ref, wvret_ref, wnt_ref, cw_ref, cb_ref = refs[0:8]
    seg_refs = refs[8:8 + len(SEGMENTS)]
    vml_ref, vret_ref, gates_ref, gatest_ref, h_ref, cbuf = refs[8 + len(SEGMENTS):]
    tm = x_ref.shape[0]

    @pl.when(pl.program_id(0) % tiles_per_seq == 0)
    def _():
        cbuf[0:TAIL, :] = jnp.zeros((TAIL, cbuf.shape[1]), F32)

    def store_chunks(ref, rows, value):
        for c in range(tm // CHUNK):
            ref[c, rows, :] = value[:, c * CHUNK:(c + 1) * CHUNK]

    x = x_ref[...]
    y = x * lax.rsqrt(jnp.mean(x * x, axis=-1, keepdims=True) + NORM_EPS) * g_ref[...]
    h_ref[...] = y.astype(BF16)
    gates_t = _dot_nt(wnt_ref[...], h_ref[...])
    store_chunks(gatest_ref, slice(0, GATE_ROWS), gates_t)
    padded = jnp.concatenate([gates_t, jnp.zeros((GATE_LANES - GATE_ROWS, tm), F32)], axis=0)
    for c in range(tm // CHUNK):
        gates_ref[c * CHUNK:(c + 1) * CHUNK, :] = padded[:, c * CHUNK:(c + 1) * CHUNK].T

    def conv_epilogue(acc, o_ref, cols, cc, scaled):
        cbuf[TAIL:TAIL + tm, cc] = acc
        out = cb_ref[:, cc] + cw_ref[CONV_K - 1:CONV_K, cc] * acc
        for j in range(1, CONV_K):
            out = out + cw_ref[CONV_K - 1 - j:CONV_K - j, cc] * cbuf[TAIL - j:TAIL - j + tm, cc]
        cbuf[0:TAIL, cc] = cbuf[tm:tm + TAIL, cc]
        out = _silu(out)
        if scaled:
            out = out * (ML_HEAD_DIM ** -0.5)
        o_ref[:, cols] = out.astype(BF16)

    def plain_epilogue(acc, o_ref, cols, act):
        o_ref[:, cols] = (acc if act is None else act(acc)).astype(BF16)

    def t_epilogue(acc, ref, rows):
        store_chunks(ref, rows, acc.astype(BF16))

    heavy, light = [], []
    conv_col = 0
    start = 0
    for o_ref, (_, width, kind) in zip(seg_refs, SEGMENTS):
        for blk in range(0, width, PROJ_BLOCK):
            cols = slice(blk, blk + PROJ_BLOCK)
            wcols = slice(start + blk, start + blk + PROJ_BLOCK)
            mm = functools.partial(lambda c: _dot(h_ref[...], w_ref[:, c]), wcols)
            if kind in ("conv", "conv_scaled"):
                cc = slice(conv_col, conv_col + PROJ_BLOCK)
                conv_col += PROJ_BLOCK
                heavy.append((mm, functools.partial(conv_epilogue, o_ref=o_ref, cols=cols, cc=cc,
                                                    scaled=(kind == "conv_scaled"))))
            else:
                act = {"silu": _silu, "sigmoid": _sigmoid, "plain": None}[kind]
                light.append((mm, functools.partial(plain_epilogue, o_ref=o_ref, cols=cols, act=act)))
        start += width
    for w_t_ref, out_ref, width in ((wvml_ref, vml_ref, ML_WIDTH), (wvret_ref, vret_ref, RET_WIDTH)):
        for blk in range(0, width, PROJ_BLOCK):
            rows = slice(blk, blk + PROJ_BLOCK)
            mm = functools.partial(lambda r, w: _dot_nt(w[r, :], h_ref[...]), rows, w_t_ref)
            light.append((mm, functools.partial(t_epilogue, ref=out_ref, rows=rows)))

    jobs = []
    while heavy or light:
        if heavy:
            jobs.append(heavy.pop(0))
        if light:
            jobs.append(light.pop(0))
        if light and len(light) > len(heavy):
            jobs.append(light.pop(0))
    acc = jobs[0][0]()
    for i, (_, epilogue) in enumerate(jobs):
        nxt = jobs[i + 1][0]() if i + 1 < len(jobs) else None
        epilogue(acc)
        acc = nxt


def _layer_spec(layer, shape, **kwargs):
    return pl.BlockSpec((None,) + tuple(shape), lambda *_: (layer,) + (0,) * len(shape), **kwargs)


def _inproj(layer, x, g, w_wide, w_vml_t, w_vret_t, w_nar_t, conv_w, conv_b, seq):
    n = x.shape[0]
    tm = min(seq, 512)
    assert seq % tm == 0 and n % seq == 0 and tm % CHUNK == 0
    rows = lambda i: (i, 0)
    chunks = lambda i: (i, 0, 0)
    cpt = tm // CHUNK
    resident = functools.partial(_layer_spec, layer, pipeline_mode=pl.Buffered(1))
    return pl.pallas_call(
        functools.partial(_inproj_kernel, tiles_per_seq=seq // tm),
        grid=(n // tm,),
        in_specs=[pl.BlockSpec((tm, D_MODEL), rows),
                  _layer_spec(layer, (1, D_MODEL)),
                  resident((D_MODEL, WIDE_WIDTH)),
                  resident((ML_WIDTH, D_MODEL)),
                  resident((RET_WIDTH, D_MODEL)),
                  resident((GATE_ROWS, D_MODEL)),
                  _layer_spec(layer, (CONV_K, CONV_WIDTH)),
                  _layer_spec(layer, (1, CONV_WIDTH))],
        out_specs=[pl.BlockSpec((tm, width), rows) for _, width, _ in SEGMENTS]
                  + [pl.BlockSpec((cpt, ML_WIDTH, CHUNK), chunks),
                     pl.BlockSpec((cpt, RET_WIDTH, CHUNK), chunks),
                     pl.BlockSpec((tm, GATE_LANES), rows),
                     pl.BlockSpec((cpt, GATE_ROWS, CHUNK), chunks)],
        out_shape=[jax.ShapeDtypeStruct((n, width), BF16) for _, width, _ in SEGMENTS]
                  + [jax.ShapeDtypeStruct((n // CHUNK, ML_WIDTH, CHUNK), BF16),
                     jax.ShapeDtypeStruct((n // CHUNK, RET_WIDTH, CHUNK), BF16),
                     jax.ShapeDtypeStruct((n, GATE_LANES), F32),
                     jax.ShapeDtypeStruct((n // CHUNK, GATE_ROWS, CHUNK), F32)],
        scratch_shapes=[pltpu.VMEM((tm, D_MODEL), BF16),
                        pltpu.VMEM((TAIL + tm, CONV_WIDTH), F32)],
        compiler_params=pltpu.CompilerParams(
            dimension_semantics=("arbitrary",), vmem_limit_bytes=VMEM_LIMIT),
        name="inproj",
    )(x, g, w_wide, w_vml_t, w_vret_t, w_nar_t, conv_w, conv_b)


def _mlstm_body(q_ref, k_ref, o_ref, gates_ref, vt_refs, gatest_refs, brow_ref, bcol_ref, norm_ref,
                y_ref, ct_ref, n_ref, m_ref):
    bsz = len(vt_refs)

    @pl.when(pl.program_id(0) == 0)
    def _():
        ct_ref[...] = jnp.zeros(ct_ref.shape, F32)
        n_ref[...] = jnp.zeros(n_ref.shape, F32)
        m_ref[...] = jnp.zeros(m_ref.shape, F32)

    row, col = _chunk_iotas()
    causal_t = row <= col
    tri = (col <= row).astype(F32)
    tri_t = causal_t.astype(F32)

    gate_terms = []
    for b in range(bsz):
        g_col = gates_ref[b] + brow_ref[...]
        g_row = gatest_refs[b][0:ML_GATE_ROWS, :] + bcol_ref[0:ML_GATE_ROWS, :]
        b_col = _dot_exact(tri, _log_sigmoid(g_col))
        b_row = _dot_exact(_log_sigmoid(g_row), tri_t)
        gate_terms.append((g_row, b_row, g_col - pltpu.roll(b_col, GATE_LANES - F_LANE, 1)))

    units = [(b, h) for b in range(bsz) for h in range(ML_HEADS)]
    head = lambda h: slice(h * ML_HEAD_DIM, (h + 1) * ML_HEAD_DIM)

    stage1 = []
    for st, (b, h) in enumerate(units):
        g_row, b_row, cs_all = gate_terms[b]
        q = q_ref[b, :, head(h)]
        k = k_ref[b, :, head(h)]
        m_prev = m_ref[st][:, 0:1]
        b_t = b_row[F_LANE + h:F_LANE + h + 1, :]
        cs = cs_all[:, h:h + 1]
        log_d = jnp.where(causal_t, b_t + cs, -jnp.inf)
        a = b_t + m_prev
        m_t = jnp.maximum(a, jnp.max(log_d, axis=0, keepdims=True))
        inter = jnp.exp(a - m_t)
        kcn = _dot_nt(jnp.concatenate([k, ct_ref[st].astype(BF16), _split_rows(n_ref[st])], axis=0), q)
        scores = kcn[0:CHUNK] * jnp.exp(log_d - m_t)
        n_q = jnp.sum(kcn[CHUNK + ML_HEAD_DIM:], axis=0, keepdims=True)
        den = jnp.sum(scores, axis=0, keepdims=True) + inter * n_q
        scale = 1.0 / jnp.maximum(jnp.abs(den), jnp.exp(-m_t))
        cross = kcn[CHUNK:CHUNK + ML_HEAD_DIM] * (inter * scale)
        stage1.append((scores.astype(BF16), cross, scale))

    outs = []
    for st, (b, h) in enumerate(units):
        scores, cross, scale = stage1[st]
        out = _dot(vt_refs[b][head(h), :], scores) * scale + cross
        outs.append(out * lax.rsqrt(jnp.mean(out * out, axis=0, keepdims=True) + NORM_EPS))

    for st, (b, h) in enumerate(units):
        g_row, b_row, _ = gate_terms[b]
        k = k_ref[b, :, head(h)]
        vt = vt_refs[b][head(h), :]
        m_prev = m_ref[st][:, 0:1]
        b_t = b_row[F_LANE + h:F_LANE + h + 1, :]
        i_t = g_row[I_LANE + h:I_LANE + h + 1, :]
        b_last = b_t[:, CHUNK - 1:CHUNK]
        w_state = b_last - b_t + i_t
        m_new = jnp.maximum(b_last + m_prev, jnp.max(w_state, axis=1, keepdims=True))
        a_old = jnp.exp(b_last + m_prev - m_new)
        e = jnp.exp(w_state - m_new)
        upd = _dot(jnp.concatenate([(vt.astype(F32) * e).astype(BF16), _split_rows(e)], axis=0), k)
        ct_ref[st] = a_old * ct_ref[st] + upd[0:ML_HEAD_DIM]
        n_ref[st] = a_old * n_ref[st] + jnp.sum(upd[ML_HEAD_DIM:], axis=0, keepdims=True)
        m_ref[st] = jnp.broadcast_to(m_new, m_ref.shape[1:])

    for st, (b, h) in enumerate(units):
        y_ref[b, :, Y_ML + h * ML_HEAD_DIM:Y_ML + (h + 1) * ML_HEAD_DIM] = (
            outs[st].T * norm_ref[:, head(h)] * o_ref[b, :, head(h)].astype(F32)).astype(BF16)


def _per_batch_specs(rows, bsz, nc):
    return [pl.BlockSpec((None, rows, CHUNK), functools.partial(lambda b, c: (b * nc + c, 0, 0), b))
            for b in range(bsz)]


def _ssd_body(d_ref, x_ref, b_ref, c_ref, z_ref, gates_ref, gatest_refs, brow_ref, bcol_ref,
              alog_row_ref, alog_col_ref, norm_ref, y_ref, st_ref, xt_buf, yt_buf, layer):
    bsz = len(gatest_refs)

    @pl.when(pl.program_id(0) == 0)
    def _():
        st_ref[...] = jnp.zeros(st_ref.shape, F32)

    row, col = _chunk_iotas()
    causal_t = row <= col
    tri = (col <= row).astype(F32)
    tri_t = causal_t.astype(F32)
    rows16 = slice(DT_ROW0, DT_ROW0 + 16)
    a_neg_row = -jnp.exp(alog_row_ref[...])
    a_neg_col = -jnp.exp(alog_col_ref[rows16, :])

    gate_terms = []
    group_terms = []
    for b in range(bsz):
        dt_col = _softplus(gates_ref[b] + brow_ref[...])
        dt_row = _softplus(gatest_refs[b][rows16, :] + bcol_ref[rows16, :])
        acum_col = _dot_exact(tri, dt_col * a_neg_row)
        acum_row = _dot_exact(dt_row * a_neg_col, tri_t)
        gate_terms.append((dt_row, acum_col, acum_row))
        for blk in range(0, SSD_WIDTH, 128):
            xt_buf[b, blk:blk + 128, :] = x_ref[b, :, blk:blk + 128].astype(F32).T
        for g in range(SSD_GROUPS):
            gs = slice(g * SSD_STATE, (g + 1) * SSD_STATE)
            b_g = b_ref[b, :, gs]
            c_g = c_ref[b, :, gs]
            prev = st_ref[b * SSD_GROUPS + g]
            bp = _dot_nt(jnp.concatenate([b_g, prev.astype(BF16)], axis=0), c_g)
            group_terms.append((b_g, prev, bp[0:CHUNK],
                                bp[CHUNK:]))

    units = [(b, h) for b in range(bsz) for h in range(SSD_HEADS)]
    head_terms = []
    for b, h in units:
        dt_row, acum_col, acum_row = gate_terms[b]
        cb = group_terms[b * SSD_GROUPS + h // SSD_HEADS_PER_GROUP][2]
        r = DT_LANE + h - DT_ROW0
        lane = DT_LANE + h
        a_t = acum_row[r:r + 1, :]
        a_s = acum_col[:, lane:lane + 1]
        l_dec = jnp.exp(jnp.where(causal_t, a_t - a_s, -jnp.inf))
        x_h = xt_buf[b, h * SSD_HEAD_DIM:(h + 1) * SSD_HEAD_DIM, :]
        xc = x_h * dt_row[r:r + 1, :]
        a_last = a_t[:, CHUNK - 1:CHUNK]
        head_terms.append(((cb * l_dec).astype(BF16), x_h, xc.astype(BF16),
                           (xc * jnp.exp(a_last - a_t)).astype(BF16), jnp.exp(a_t), jnp.exp(a_last)))

    for i, (b, h) in enumerate(units):
        m_h, x_h, xc_b, _, exp_a, _ = head_terms[i]
        g, hh = divmod(h, SSD_HEADS_PER_GROUP)
        y_off = group_terms[b * SSD_GROUPS + g][3]
        hs = slice(h * SSD_HEAD_DIM, (h + 1) * SSD_HEAD_DIM)
        yt_buf[b, hs, :] = (_dot(xc_b, m_h)
                            + y_off[hh * SSD_HEAD_DIM:(hh + 1) * SSD_HEAD_DIM, :] * exp_a
                            + d_ref[layer, h] * x_h)

    for b in range(bsz):
        for g in range(SSD_GROUPS):
            st = b * SSD_GROUPS + g
            b_g, prev = group_terms[st][0:2]
            terms = head_terms[b * SSD_HEADS + g * SSD_HEADS_PER_GROUP:
                               b * SSD_HEADS + (g + 1) * SSD_HEADS_PER_GROUP]
            upd = _dot(jnp.concatenate([t[3] for t in terms], axis=0), b_g)
            for hh in range(SSD_HEADS_PER_GROUP):
                ps = slice(hh * SSD_HEAD_DIM, (hh + 1) * SSD_HEAD_DIM)
                st_ref[st, ps, :] = terms[hh][5] * prev[ps, :] + upd[ps, :]

    for b in range(bsz):
        for g in range(SSD_GROUPS):
            parts = []
            for blk in range(g * SSD_GROUP_WIDTH, (g + 1) * SSD_GROUP_WIDTH, 128):
                parts.append(yt_buf[b, blk:blk + 128, :].T * z_ref[b, :, blk:blk + 128].astype(F32))
            ssq = sum(jnp.sum(p * p, axis=1, keepdims=True) for p in parts)
            inv = lax.rsqrt(ssq * (1.0 / SSD_GROUP_WIDTH) + NORM_EPS)
            for i, p in enumerate(parts):
                cs = slice(g * SSD_GROUP_WIDTH + i * 128, g * SSD_GROUP_WIDTH + (i + 1) * 128)
                y_ref[b, :, Y_SSD + cs.start:Y_SSD + cs.stop] = (p * inv * norm_ref[:, cs]).astype(BF16)


def _ret_body(q_ref, k_ref, g_ref, cc_ref, ss_ref, vt_refs, norm_ref, y_ref, rt_ref, dmat_ref):
    bsz = len(vt_refs)
    log_gammas = [math.log1p(-2.0 ** (-5.0 - h)) for h in range(RET_HEADS)]

    @pl.when(pl.program_id(0) == 0)
    def _():
        rt_ref[...] = jnp.zeros(rt_ref.shape, F32)
        row, col = _chunk_iotas()
        rel = (col - row).astype(F32)
        for h in range(RET_HEADS):
            dmat_ref[h] = jnp.exp(jnp.where(row <= col, log_gammas[h] * rel, -jnp.inf))

    t_row = lax.broadcasted_iota(jnp.int32, (1, CHUNK), 1).astype(F32)
    half = RET_HEAD_DIM // 2
    scale = RET_HEAD_DIM ** -0.5

    roped = []
    for b in range(bsz):
        cc = cc_ref[b]
        ss = ss_ref[b]
        for h in range(RET_HEADS):
            sl = slice(h * RET_HEAD_DIM, (h + 1) * RET_HEAD_DIM)
            q = q_ref[b, :, sl].astype(F32)
            k = k_ref[b, :, sl].astype(F32)
            roped.append(((q * cc + pltpu.roll(q, half, 1) * ss).astype(BF16),
                          ((k * cc + pltpu.roll(k, half, 1) * ss) * scale).astype(BF16)))

    units = [(b, h) for b in range(bsz) for h in range(RET_HEADS)]
    xis = [jnp.exp(lg * (t_row + 1.0)) for lg in log_gammas]
    zetas = [jnp.exp(lg * (CHUNK - 1.0 - t_row)) for lg in log_gammas]
    scores, cross = [], []
    for st, (b, h) in enumerate(units):
        q, k = roped[st]
        kr = _dot_nt(jnp.concatenate([k, rt_ref[st].astype(BF16)], axis=0), q)
        scores.append((kr[0:CHUNK] * dmat_ref[h]).astype(BF16))
        cross.append(kr[CHUNK:] * xis[h])
    ys = []
    for st, (b, h) in enumerate(units):
        sl = slice(h * RET_HEAD_DIM, (h + 1) * RET_HEAD_DIM)
        _, k = roped[st]
        vt = vt_refs[b][sl, :]
        ys.append(_dot(vt, scores[st]) + cross[st])
        rt_ref[st] = (math.exp(log_gammas[h] * CHUNK) * rt_ref[st]
                      + _dot((vt.astype(F32) * zetas[h]).astype(BF16), k))
    for st, (b, h) in enumerate(units):
        sl = slice(h * RET_HEAD_DIM, (h + 1) * RET_HEAD_DIM)
        y = ys[st]
        yc = y - jnp.mean(y, axis=0, keepdims=True)
        yn = yc * lax.rsqrt(jnp.mean(yc * yc, axis=0, keepdims=True) + NORM_EPS)
        y_ref[b, :, Y_RET + sl.start:Y_RET + sl.stop] = (
            yn.T * norm_ref[:, sl] * g_ref[b, :, sl].astype(F32)).astype(BF16)


def _mixers_kernel(*refs, bsz, layer):
    it = iter(refs)
    take = lambda k=None: next(it) if k is None else [next(it) for _ in range(k)]
    d_ref = take()
    ml_q, ml_k, ml_o, ssd_x, ssd_b, ssd_c, ssd_z, ret_q, ret_k, ret_g, gates_ref, cc_ref, ss_ref = take(13)
    vml_refs, vret_refs, gatest_refs = take(bsz), take(bsz), take(bsz)
    brow_ref, bcol_ref, alog_row_ref, alog_col_ref, ml_norm_ref, ssd_norm_ref, ret_norm_ref = take(7)
    y_ref = take()
    ct_ref, n_ref, m_ref, st_ref, xt_buf, yt_buf, rt_ref, dmat_ref = take(8)
    _mlstm_body(ml_q, ml_k, ml_o, gates_ref, vml_refs, gatest_refs, brow_ref, bcol_ref, ml_norm_ref,
                y_ref, ct_ref, n_ref, m_ref)
    _ssd_body(d_ref, ssd_x, ssd_b, ssd_c, ssd_z, gates_ref, gatest_refs, brow_ref, bcol_ref,
              alog_row_ref, alog_col_ref, ssd_norm_ref, y_ref, st_ref, xt_buf, yt_buf, layer)
    _ret_body(ret_q, ret_k, ret_g, cc_ref, ss_ref, vret_refs, ret_norm_ref, y_ref, rt_ref, dmat_ref)


def _mixers(layer, d, segs, gates, cc, ss, vml_t, vret_t, gates_t, bias_row, bias_col, alog_row,
            alog_col, ml_norm, ssd_norm, ret_norm, bsz, nc):
    tok3 = lambda c: (0, c, 0)
    names = ("ml_q", "ml_k", "ml_o", "ssd_x", "ssd_b", "ssd_c", "ssd_z", "ret_q", "ret_k", "ret_g")
    seg_args = [segs[SEG_INDEX[name]] for name in names]
    return pl.pallas_call(
        functools.partial(_mixers_kernel, bsz=bsz, layer=layer),
        grid=(nc,),
        in_specs=[pl.BlockSpec(memory_space=pltpu.SMEM)]
                 + [pl.BlockSpec((bsz, CHUNK, a.shape[-1]), tok3) for a in seg_args + [gates, cc, ss]]
                 + _per_batch_specs(ML_WIDTH, bsz, nc) + _per_batch_specs(RET_WIDTH, bsz, nc)
                 + _per_batch_specs(GATE_ROWS, bsz, nc)
                 + [_layer_spec(layer, (1, GATE_LANES)),
                    _layer_spec(layer, (GATE_ROWS, 1)),
                    _layer_spec(layer, (1, GATE_LANES)),
                    _layer_spec(layer, (GATE_ROWS, 1)),
                    _layer_spec(layer, (1, ML_WIDTH)),
                    _layer_spec(layer, (1, SSD_WIDTH)),
                    _layer_spec(layer, (1, RET_WIDTH))],
        out_specs=pl.BlockSpec((bsz, CHUNK, MIX_WIDTH), tok3),
        out_shape=jax.ShapeDtypeStruct((bsz, nc * CHUNK, MIX_WIDTH), BF16),
        scratch_shapes=[pltpu.VMEM((bsz * ML_HEADS, ML_HEAD_DIM, ML_HEAD_DIM), F32),
                        pltpu.VMEM((bsz * ML_HEADS, 1, ML_HEAD_DIM), F32),
                        pltpu.VMEM((bsz * ML_HEADS, 1, 128), F32),
                        pltpu.VMEM((bsz * SSD_GROUPS, SSD_GROUP_WIDTH, SSD_STATE), F32),
                        pltpu.VMEM((bsz, SSD_WIDTH, CHUNK), F32),
                        pltpu.VMEM((bsz, SSD_WIDTH, CHUNK), F32),
                        pltpu.VMEM((bsz * RET_HEADS, RET_HEAD_DIM, RET_HEAD_DIM), F32),
                        pltpu.VMEM((RET_HEADS, CHUNK, CHUNK), F32)],
        compiler_params=pltpu.CompilerParams(
            dimension_semantics=("arbitrary",), vmem_limit_bytes=VMEM_LIMIT),
        name="mixers",
    )(d, *seg_args, gates, cc, ss, *([vml_t] * bsz), *([vret_t] * bsz), *([gates_t] * bsz),
      bias_row, bias_col, alog_row, alog_col, ml_norm, ssd_norm, ret_norm)


FFN_BLOCK = 256


def _out_ffn_kernel(x_ref, mix_ref, wo_ref, g_ref, wgu_ref, wd_ref, gfin_ref, o_ref, *, final_norm):
    x1 = x_ref[...] + _dot(mix_ref[...], wo_ref[...])
    h = (x1 * lax.rsqrt(jnp.mean(x1 * x1, axis=-1, keepdims=True) + NORM_EPS) * g_ref[...]).astype(BF16)
    acc = x1
    for blk in range(0, FFN_HIDDEN, FFN_BLOCK):
        gate = _dot(h, wgu_ref[:, blk:blk + FFN_BLOCK])
        up = _dot(h, wgu_ref[:, FFN_HIDDEN + blk:FFN_HIDDEN + blk + FFN_BLOCK])
        acc = acc + _dot((_silu(gate) * up).astype(BF16), wd_ref[blk:blk + FFN_BLOCK, :])
    if final_norm:
        acc = acc * lax.rsqrt(jnp.mean(acc * acc, axis=-1, keepdims=True) + NORM_EPS) * gfin_ref[...]
    o_ref[...] = acc


def _out_ffn(layer, x, mix, w_out, g, w_gu, w_down, g_final, final_norm):
    n = x.shape[0]
    tm = min(n, 1024)
    assert n % tm == 0
    rows = lambda i: (i, 0)
    const = lambda i: (0, 0)
    resident = functools.partial(_layer_spec, layer, pipeline_mode=pl.Buffered(1))
    return pl.pallas_call(
        functools.partial(_out_ffn_kernel, final_norm=final_norm),
        grid=(n // tm,),
        in_specs=[pl.BlockSpec((tm, D_MODEL), rows),
                  pl.BlockSpec((tm, MIX_WIDTH), rows),
                  resident((MIX_WIDTH, D_MODEL)),
                  _layer_spec(layer, (1, D_MODEL)),
                  resident((D_MODEL, 2 * FFN_HIDDEN)),
                  resident((FFN_HIDDEN, D_MODEL)),
                  pl.BlockSpec((1, D_MODEL), const)],
        out_specs=pl.BlockSpec((tm, D_MODEL), rows),
        out_shape=jax.ShapeDtypeStruct((n, D_MODEL), F32),
        compiler_params=pltpu.CompilerParams(
            dimension_semantics=("parallel",), vmem_limit_bytes=VMEM_LIMIT),
        name="out_ffn",
    )(x, mix, w_out, g, w_gu, w_down, g_final)


def _gate_row(v, offset):
    depth, k = v.shape
    return jnp.zeros((depth, 1, GATE_LANES), F32).at[:, 0, offset:offset + k].set(v)


def _gate_col(v, offset):
    depth, k = v.shape
    return jnp.zeros((depth, GATE_ROWS, 1), F32).at[:, offset:offset + k, 0].set(v)


def kernel(x, positions, norm_mix, w_in, ml_conv_w, ml_conv_b, ml_gate_bias, ml_norm,
           ssd_conv_w, ssd_conv_b, ssd_dt_bias, ssd_a_log, ssd_d, ssd_norm, ret_norm,
           w_out, norm_ffn, w_gate_up, w_down, norm_final):
    bsz, seq, _ = x.shape
    depth = w_in.shape[0]
    n = bsz * seq
    nc = seq // CHUNK
    assert seq % CHUNK == 0 and x.shape[2] == D_MODEL and w_in.shape[2] == ML_IN + SSD_IN + RET_IN

    ml0, ssd0, ret0 = 0, ML_IN, ML_IN + SSD_IN
    w_in_b = w_in.astype(BF16)
    seg = lambda start, width: w_in_b[:, :, start:start + width]
    ml_q, ml_k = seg(ml0, ML_WIDTH), seg(ml0 + ML_WIDTH, ML_WIDTH)
    ml_v, ml_o = seg(ml0 + 2 * ML_WIDTH, ML_WIDTH), seg(ml0 + 3 * ML_WIDTH, ML_WIDTH)
    ml_if = seg(ml0 + 4 * ML_WIDTH, 2 * ML_HEADS)
    ssd_z, ssd_x = seg(ssd0, SSD_WIDTH), seg(ssd0 + SSD_WIDTH, SSD_WIDTH)
    ssd_bc = seg(ssd0 + 2 * SSD_WIDTH, 2 * SSD_BC_WIDTH)
    ssd_dt = seg(ssd0 + SSD_WIDTH + SSD_CONV_DIM, SSD_HEADS)
    ret_qk, ret_v = seg(ret0, 2 * RET_WIDTH), seg(ret0 + 2 * RET_WIDTH, RET_WIDTH)
    ret_g = seg(ret0 + 3 * RET_WIDTH, RET_WIDTH)
    w_wide = jnp.concatenate([ret_qk, ret_g, ml_q, ml_k, ssd_x, ml_o, ssd_z, ssd_bc], axis=-1)
    w_vml_t = jnp.swapaxes(ml_v, 1, 2)
    w_vret_t = jnp.swapaxes(ret_v, 1, 2)
    w_gate_cols = jnp.concatenate([ml_if, ssd_dt], axis=-1)
    n_gate = w_gate_cols.shape[-1]
    w_nar_t = jnp.pad(jnp.swapaxes(w_gate_cols, 1, 2), ((0, 0), (0, GATE_ROWS - n_gate), (0, 0)))
    conv_w = jnp.concatenate([ml_conv_w, ssd_conv_w], axis=-1)
    conv_b = jnp.concatenate([ml_conv_b, ssd_conv_b], axis=-1)[:, None, :]
    w_out_b = w_out.astype(BF16)
    w_gu_b = w_gate_up.astype(BF16)
    w_down_b = w_down.astype(BF16)

    gate_bias = jnp.concatenate([ml_gate_bias, ssd_dt_bias], axis=-1)
    bias_row, bias_col = _gate_row(gate_bias, 0), _gate_col(gate_bias, 0)
    alog_row, alog_col = _gate_row(ssd_a_log, DT_LANE), _gate_col(ssd_a_log, DT_LANE)
    row3 = lambda v: v[:, None, :]
    norm_mix3, norm_ffn3 = row3(norm_mix), row3(norm_ffn)
    ml_norm3, ssd_norm3, ret_norm3 = row3(ml_norm), row3(ssd_norm), row3(ret_norm)
    g_final = norm_final.reshape(1, D_MODEL)

    cc, ss = _rope_tables(positions)
    cc = cc.reshape(bsz, seq, RET_HEAD_DIM)
    ss = ss.reshape(bsz, seq, RET_HEAD_DIM)
    xf = x.reshape(n, D_MODEL)
    for l in range(depth):
        *segs, vml_t, vret_t, gates, gates_t = _inproj(
            l, xf, norm_mix3, w_wide, w_vml_t, w_vret_t, w_nar_t, conv_w, conv_b, seq)
        segs = [a.reshape(bsz, seq, a.shape[-1]) for a in segs]
        gates = gates.reshape(bsz, seq, GATE_LANES)
        mix = _mixers(l, ssd_d, segs, gates, cc, ss, vml_t, vret_t, gates_t, bias_row, bias_col,
                      alog_row, alog_col, ml_norm3, ssd_norm3, ret_norm3, bsz, nc)
        xf = _out_ffn(l, xf, mix.reshape(n, MIX_WIDTH), w_out_b, norm_ffn3, w_gu_b, w_down_b,
                      g_final, final_norm=(l == depth - 1))
    return xf.reshape(bsz, seq, D_MODEL)
```
